```python
import jax, jax.numpy as jnp
from jax import lax
import numpy as np

D_MODEL = 1024
BATCH = 32
SEQ = 2048
DEPTH = 2

HEAD_DIM = 64
N_ATTN_HEADS = 8
N_KV_HEADS = 2
GQA_GROUP = N_ATTN_HEADS // N_KV_HEADS
ATTN_WIDTH = N_ATTN_HEADS * HEAD_DIM
KV_WIDTH = N_KV_HEADS * HEAD_DIM
ROPE_DIM = HEAD_DIM // 4
ROPE_THETA = 500000.0
DILATED_PATTERNS = ((128, 1), (512, 4), (2048, 16))
ATTN_BLOCK = 128

SSM_HEAD_DIM = 64
SSM_HEADS = 16
SSM_INNER = SSM_HEADS * SSM_HEAD_DIM
SSM_GROUPS = 2
D_STATE = 128
CONV_WIDTH = 4
CHUNK = 128
CONV_CH = SSM_INNER + 2 * SSM_GROUPS * D_STATE

MIX_WIDTH = ATTN_WIDTH + SSM_INNER
Q_END = ATTN_WIDTH
K_END = Q_END + KV_WIDTH
V_END = K_END + KV_WIDTH
Z_END = V_END + SSM_INNER
XBC_END = Z_END + CONV_CH
IN_PROJ = XBC_END + SSM_HEADS

FFN_HIDDEN = ((8 * D_MODEL + 3 * 256 - 1) // (3 * 256)) * 256
EPS = 1e-5

kernel_name = "hybrid_dilated_attn_mamba2_block"


def rmsnorm(x, w):
    xf = x.astype(jnp.float32)
    y = xf * lax.rsqrt(jnp.mean(xf * xf, axis=-1, keepdims=True) + EPS)
    return (y * w.astype(jnp.float32)).astype(x.dtype)


def rotary_tables(positions, dtype):
    inv_freq = ROPE_THETA ** (-jnp.arange(0, ROPE_DIM, 2, dtype=jnp.float32) / ROPE_DIM)
    ang = positions.astype(jnp.float32)[..., None] * inv_freq
    return jnp.cos(ang)[:, :, None, :].astype(dtype), jnp.sin(ang)[:, :, None, :].astype(dtype)


def partial_rotary(t, cos, sin):
    half = ROPE_DIM // 2
    t1, t2, rest = t[..., :half], t[..., half:ROPE_DIM], t[..., ROPE_DIM:]
    return jnp.concatenate([t1 * cos - t2 * sin, t2 * cos + t1 * sin, rest], axis=-1)


def dilated_window_branch(q, k, v, window, dilation):
    bsz, s = q.shape[0], q.shape[1]
    length = s // dilation
    w_d = window // dilation
    nb = -(-length // ATTN_BLOCK)
    lp = nb * ATTN_BLOCK
    qd = q.reshape((bsz, length, dilation) + q.shape[2:])
    kd = k.reshape((bsz, length, dilation) + k.shape[2:])
    vd = v.reshape((bsz, length, dilation) + v.shape[2:])
    qd = jnp.pad(qd, [(0, 0), (0, lp - length)] + [(0, 0)] * (qd.ndim - 2))
    kv_pad = [(0, 0), (ATTN_BLOCK, lp - length)] + [(0, 0)] * (kd.ndim - 2)
    kd = jnp.pad(kd, kv_pad)
    vd = jnp.pad(vd, kv_pad)
    qb = qd.reshape((bsz, nb, ATTN_BLOCK) + qd.shape[2:])
    kb = kd.reshape((bsz, nb + 1, ATTN_BLOCK) + kd.shape[2:])
    vb = vd.reshape((bsz, nb + 1, ATTN_BLOCK) + vd.shape[2:])
    kb = jnp.concatenate([kb[:, :-1], kb[:, 1:]], axis=2)
    vb = jnp.concatenate([vb[:, :-1], vb[:, 1:]], axis=2)
    scores = jnp.einsum('bnqrhgc,bnkrhc->bnrhgqk', qb, kb).astype(jnp.float32)
    qi = jnp.arange(ATTN_BLOCK)[:, None]
    ki = jnp.arange(2 * ATTN_BLOCK)[None, :]
    delta = qi + ATTN_BLOCK - ki
    kpos = jnp.arange(nb)[:, None, None] * ATTN_BLOCK - ATTN_BLOCK + ki[None]
    valid = (delta >= 0)[None] & (delta <= w_d)[None] & (kpos >= 0)
    scores = jnp.where(valid[None, :, None, None, None], scores, -jnp.inf)
    m = jnp.max(scores, axis=-1, keepdims=True)
    p = jnp.exp(scores - m)
    den = jnp.sum(p, axis=-1, keepdims=True)
    o = jnp.einsum('bnrhgqk,bnkrhc->bnrhgqc', p, vb.astype(jnp.float32)) / den
    lse = (m + jnp.log(den))[..., 0]
    o = jnp.transpose(o, (0, 1, 5, 2, 3, 4, 6)).reshape((bsz, lp, dilation) + q.shape[2:])
    lse = jnp.transpose(lse, (0, 1, 5, 2, 3, 4)).reshape((bsz, lp, dilation) + q.shape[2:4])
    o = o[:, :length].reshape(q.shape)
    lse = lse[:, :length].reshape(q.shape[:4])
    return o, lse


def dilated_attention(q, k, v):
    outs, lses = [], []
    for window, dilation in DILATED_PATTERNS:
        o, l = dilated_window_branch(q, k, v, window, dilation)
        outs.append(o)
        lses.append(l)
    wts = jax.nn.softmax(jnp.stack(lses, axis=0), axis=0)
    return jnp.einsum('ibshg,ibshgc->bshgc', wts, jnp.stack(outs, axis=0))


def causal_depthwise_conv(u, w, b):
    y = lax.conv_general_dilated(u, w[:, None, :], window_strides=(1,),
                                 padding=[(CONV_WIDTH - 1, 0)],
                                 dimension_numbers=('NWC', 'WIO', 'NWC'),
                                 feature_group_count=u.shape[-1])
    return y + b


def segsum_exp(a):
    cs = jnp.cumsum(a, axis=-1)
    diff = cs[..., :, None] - cs[..., None, :]
    t = a.shape[-1]
    mask = jnp.tril(jnp.ones((t, t), dtype=bool))
    return jnp.exp(jnp.where(mask, diff, -jnp.inf))


def ssd_chunked(xs, dt, a_neg, bm, cm):
    bsz, s, nh, hp = xs.shape
    nc = s // CHUNK
    e = nh // SSM_GROUPS
    xg = (xs.astype(jnp.float32) * dt[..., None]).reshape(bsz, nc, CHUNK, SSM_GROUPS, e, hp)
    a = jnp.transpose((dt * a_neg).reshape(bsz, nc, CHUNK, SSM_GROUPS, e), (0, 1, 3, 4, 2))
    bc = bm.astype(jnp.float32).reshape(bsz, nc, CHUNK, SSM_GROUPS, D_STATE)
    cc = cm.astype(jnp.float32).reshape(bsz, nc, CHUNK, SSM_GROUPS, D_STATE)
    a_cs = jnp.cumsum(a, axis=-1)
    cb = jnp.einsum('bclgn,bcsgn->bcgls', cc, bc)
    m_mat = cb[:, :, :, None] * segsum_exp(a)
    y_diag = jnp.einsum('bcgels,bcsgep->bclgep', m_mat, xg)
    decay_states = jnp.exp(a_cs[..., -1:] - a_cs)
    states = jnp.einsum('bclgn,bcgel,bclgep->bcgepn', bc, decay_states, xg)
    chunk_decay = jnp.exp(a_cs[..., -1])

    def step(h, inp):
        dec, st = inp
        return h * dec[..., None, None] + st, h

    h0 = jnp.zeros((bsz, SSM_GROUPS, e, hp, D_STATE), jnp.float32)
    _, prev = lax.scan(step, h0, (jnp.moveaxis(chunk_decay, 1, 0), jnp.moveaxis(states, 1, 0)))
    prev_states = jnp.moveaxis(prev, 0, 1)
    y_off = jnp.einsum('bclgn,bcgepn,bcgel->bclgep', cc, prev_states, jnp.exp(a_cs))
    return (y_diag + y_off).reshape(bsz, s, nh, hp)


def hybrid_mixer(h, w_in, conv_w, conv_b, dt_bias, a_log, d_skip, ssm_norm, w_out, cos, sin):
    bsz, s, _ = h.shape
    proj = h @ w_in
    q, k, v, z, xbc, dt = jnp.split(proj, [Q_END, K_END, V_END, Z_END, XBC_END], axis=-1)
    q = partial_rotary(q.reshape(bsz, s, N_ATTN_HEADS, HEAD_DIM), cos, sin)
    k = partial_rotary(k.reshape(bsz, s, N_KV_HEADS, HEAD_DIM), cos, sin)
    v = v.reshape(bsz, s, N_KV_HEADS, HEAD_DIM)
    q = (q * (HEAD_DIM ** -0.5)).reshape(bsz, s, N_KV_HEADS, GQA_GROUP, HEAD_DIM)
    attn = dilated_attention(q, k, v).reshape(bsz, s, ATTN_WIDTH).astype(h.dtype)
    xbc = jax.nn.silu(causal_depthwise_conv(xbc, conv_w, conv_b))
    xs, bm, cm = jnp.split(xbc, [SSM_INNER, SSM_INNER + SSM_GROUPS * D_STATE], axis=-1)
    xs = xs.reshape(bsz, s, SSM_HEADS, SSM_HEAD_DIM)
    bm = bm.reshape(bsz, s, SSM_GROUPS, D_STATE)
    cm = cm.reshape(bsz, s, SSM_GROUPS, D_STATE)
    dt = jax.nn.softplus(dt.astype(jnp.float32) + dt_bias.astype(jnp.float32))
    a_neg = -jnp.exp(a_log.astype(jnp.float32))
    y = ssd_chunked(xs, dt, a_neg, bm, cm) + d_skip.astype(jnp.float32)[:, None] * xs.astype(jnp.float32)
    y = y.reshape(bsz, s, SSM_INNER).astype(h.dtype) * jax.nn.silu(z)
    gsize = SSM_INNER // SSM_GROUPS
    y = rmsnorm(y.reshape(bsz, s, SSM_GROUPS, gsize), ssm_norm.reshape(SSM_GROUPS, gsize))
    y = y.reshape(bsz, s, SSM_INNER)
    return jnp.concatenate([attn, y], axis=-1) @ w_out


def swiglu(h, w_gate, w_up, w_down):
    return (jax.nn.silu(h @ w_gate) * (h @ w_up)) @ w_down


def setup_inputs(seed: int = 0) -> dict:
    key = jax.random.key(seed)
    ks = jax.random.split(key, 16)
    f32 = jnp.float32
    x = jax.random.normal(ks[0], (BATCH, SEQ, D_MODEL), f32)
    positions = jnp.broadcast_to(jnp.arange(SEQ, dtype=jnp.int32), (BATCH, SEQ))
    norm_mix = 1.0 + 0.02 * jax.random.normal(ks[1], (DEPTH, D_MODEL), f32)
    w_in = jax.random.normal(ks[2], (DEPTH, D_MODEL, IN_PROJ), f32) * D_MODEL ** -0.5
    conv_w = jax.random.normal(ks[3], (DEPTH, CONV_WIDTH, CONV_CH), f32) * CONV_WIDTH ** -0.5
    conv_b = 0.01 * jax.random.normal(ks[4], (DEPTH, CONV_CH), f32)
    dt0 = jnp.exp(jax.random.uniform(ks[5], (DEPTH, SSM_HEADS), f32, np.log(1e-3), np.log(1e-1)))
    dt_bias = dt0 + jnp.log(-jnp.expm1(-dt0))
    a_log = jnp.log(jax.random.uniform(ks[6], (DEPTH, SSM_HEADS), f32, 1.0, 16.0))
    d_skip = 1.0 + 0.1 * jax.random.normal(ks[7], (DEPTH, SSM_HEADS), f32)
    ssm_norm = 1.0 + 0.02 * jax.random.normal(ks[8], (DEPTH, SSM_INNER), f32)
    w_out = jax.random.normal(ks[9], (DEPTH, MIX_WIDTH, D_MODEL), f32) * MIX_WIDTH ** -0.5
    norm_ffn = 1.0 + 0.02 * jax.random.normal(ks[10], (DEPTH, D_MODEL), f32)
    w_gate = jax.random.normal(ks[11], (DEPTH, D_MODEL, FFN_HIDDEN), f32) * D_MODEL ** -0.5
    w_up = jax.random.normal(ks[12], (DEPTH, D_MODEL, FFN_HIDDEN), f32) * D_MODEL ** -0.5
    w_down = jax.random.normal(ks[13], (DEPTH, FFN_HIDDEN, D_MODEL), f32) * FFN_HIDDEN ** -0.5
    final_norm = 1.0 + 0.02 * jax.random.normal(ks[14], (D_MODEL,), f32)
    return {"x": x, "positions": positions, "norm_mix": norm_mix, "w_in": w_in,
            "conv_w": conv_w, "conv_b": conv_b, "dt_bias": dt_bias, "a_log": a_log,
            "d_skip": d_skip, "ssm_norm": ssm_norm, "w_out": w_out, "norm_ffn": norm_ffn,
            "w_gate": w_gate, "w_up": w_up, "w_down": w_down, "final_norm": final_norm}


def reference(x, positions, norm_mix, w_in, conv_w, conv_b, dt_bias, a_log, d_skip, ssm_norm,
              w_out, norm_ffn, w_gate, w_up, w_down, final_norm):
    cos, sin = rotary_tables(positions, x.dtype)
    h = x
    for layer in range(DEPTH):
        h = h + hybrid_mixer(rmsnorm(h, norm_mix[layer]), w_in[layer], conv_w[layer],
                             conv_b[layer], dt_bias[layer], a_log[layer], d_skip[layer],
                             ssm_norm[layer], w_out[layer], cos, sin)
        h = h + swiglu(rmsnorm(h, norm_ffn[layer]), w_gate[layer], w_up[layer], w_down[layer])
    return rmsnorm(h, final_norm)
```

```python
import functools

import jax
import jax.numpy as jnp
import numpy as np
from jax import lax
from jax.experimental import pallas as pl
from jax.experimental.pallas import tpu as pltpu

F32 = jnp.float32
BF16 = jnp.bfloat16

D_MODEL = 1024
HEAD_DIM = 64
N_ATTN_HEADS = 8
N_KV_HEADS = 2
GQA_GROUP = N_ATTN_HEADS // N_KV_HEADS
ATTN_WIDTH = N_ATTN_HEADS * HEAD_DIM
KV_WIDTH = N_KV_HEADS * HEAD_DIM
ROPE_DIM = HEAD_DIM // 4
ROPE_HALF = ROPE_DIM // 2
ROPE_THETA = 500000.0
DILATED_PATTERNS = ((128, 1), (512, 4), (2048, 16))
ATTN_BLOCK = 128
SSM_HEAD_DIM = 64
SSM_HEADS = 16
SSM_INNER = SSM_HEADS * SSM_HEAD_DIM
SSM_GROUPS = 2
HEADS_PER_GROUP = SSM_HEADS // SSM_GROUPS
GROUP_WIDTH = SSM_INNER // SSM_GROUPS
D_STATE = 128
CONV_WIDTH = 4
CHUNK = 128
BC_WIDTH = SSM_GROUPS * D_STATE
CONV_CH = SSM_INNER + 2 * BC_WIDTH
MIX_WIDTH = ATTN_WIDTH + SSM_INNER
Q_END = ATTN_WIDTH
K_END = Q_END + KV_WIDTH
V_END = K_END + KV_WIDTH
Z_END = V_END + SSM_INNER
XBC_END = Z_END + CONV_CH
FFN_HIDDEN = 2816
EPS = 1e-5

LANES = 128
SUBLANES = 8
MXU_DIM = 256
VMEM_LIMIT_BYTES = 56 * 1024 * 1024

Q_SLABS = ATTN_WIDTH // LANES
HEADS_PER_SLAB = LANES // HEAD_DIM
DT_REPLICAS = LANES // SSM_HEADS
IN_PROJ_COLS = XBC_END + LANES
NEG_INF = float("-inf")

TOKEN_TILE = 512
SSD_ROWS = 512
FFN_CHUNK = 256


def _dot(a, b):
    return jnp.dot(a, b, preferred_element_type=F32)


def _dot_nt(a, b):
    return lax.dot_general(a, b, (((1,), (1,)), ((), ())), preferred_element_type=F32)


def _split3(x):
    hi = x.astype(BF16)
    r1 = x - hi.astype(F32)
    mid = r1.astype(BF16)
    lo = (r1 - mid.astype(F32)).astype(BF16)
    return hi, mid, lo


def _silu(x):
    return x * (1.0 / (1.0 + jnp.exp(-x)))


def _softplus(x):
    return jnp.maximum(x, 0.0) + jnp.log(1.0 + jnp.exp(-jnp.abs(x)))


def _in_proj_kernel(x_ref, nw_ref, w_ref, rc_ref, rsa_ref, rsb_ref,
                    q_ref, k_ref, v_ref, z_ref, xbc_ref, dt_ref):
    x = x_ref[...]
    ms = jnp.mean(x * x, axis=-1, keepdims=True)
    hn = (x * lax.rsqrt(ms + EPS) * nw_ref[...]).astype(BF16)
    rc = rc_ref[...]
    rsa = rsa_ref[...]
    rsb = rsb_ref[...]

    def rope(t):
        return (t * rc + pltpu.roll(t, LANES - ROPE_HALF, axis=1) * rsa
                + pltpu.roll(t, ROPE_HALF, axis=1) * rsb)

    qk = _dot(hn, w_ref[:, 0:K_END])
    scale = HEAD_DIM ** -0.5
    for s in range(Q_SLABS):
        q_ref[s] = rope(qk[:, s * LANES:(s + 1) * LANES]) * scale
    k_ref[...] = rope(qk[:, Q_END:K_END])
    v_ref[...] = _dot(hn, w_ref[:, K_END:V_END])
    z_ref[...] = _dot(hn, w_ref[:, V_END:Z_END])
    xbc_ref[...] = _dot(hn, w_ref[:, Z_END:XBC_END])
    dt_ref[...] = _dot(hn, w_ref[:, XBC_END:IN_PROJ_COLS])


def _in_proj(h, norm_w, w, rc, rsa, rsb):
    n = h.shape[0]
    tm = TOKEN_TILE
    row = lambda i: (i, 0)
    const = lambda i: (0, 0)
    return pl.pallas_call(
        _in_proj_kernel,
        grid=(n // tm,),
        in_specs=[
            pl.BlockSpec((tm, D_MODEL), row),
            pl.BlockSpec((1, D_MODEL), const),
            pl.BlockSpec((D_MODEL, IN_PROJ_COLS), const),
            pl.BlockSpec((tm, LANES), row),
            pl.BlockSpec((tm, LANES), row),
            pl.BlockSpec((tm, LANES), row),
        ],
        out_specs=[
            pl.BlockSpec((Q_SLABS, tm, LANES), lambda i: (0, i, 0)),
            pl.BlockSpec((tm, KV_WIDTH), row),
            pl.BlockSpec((tm, KV_WIDTH), row),
            pl.BlockSpec((tm, SSM_INNER), row),
            pl.BlockSpec((tm, CONV_CH), row),
            pl.BlockSpec((tm, LANES), row),
        ],
        out_shape=[
            jax.ShapeDtypeStruct((Q_SLABS, n, LANES), F32),
            jax.ShapeDtypeStruct((n, KV_WIDTH), F32),
            jax.ShapeDtypeStruct((n, KV_WIDTH), F32),
            jax.ShapeDtypeStruct((n, SSM_INNER), F32),
            jax.ShapeDtypeStruct((n, CONV_CH), F32),
            jax.ShapeDtypeStruct((n, LANES), F32),
        ],
        compiler_params=pltpu.CompilerParams(
            dimension_semantics=("parallel",), vmem_limit_bytes=VMEM_LIMIT_BYTES),
        name="in_proj",
    )(h, norm_w, w, rc, rsa, rsb)


def _attn_kernel(q_ref, k_ref, v_ref, bias_ref, o_ref, acc_ref, m_ref, l_ref, *, seq):
    lane = lax.broadcasted_iota(jnp.int32, (ATTN_BLOCK, LANES), 1)
    low_half = lane < HEAD_DIM

    def rows(start, stride):
        if stride == 1:
            return pl.ds(start, ATTN_BLOCK)
        return pl.ds(start, ATTN_BLOCK, stride=stride)

    def block(q_start, prev_start, stride, init):
        q_rows = rows(q_start, stride)
        k_cur = k_ref[q_rows, :]
        v_cur = v_ref[q_rows, :]
        if prev_start is None:
            keys, vals = k_cur, v_cur
            bias = bias_ref[:, ATTN_BLOCK:]
        else:
            p_rows = rows(prev_start, stride)
            keys = jnp.concatenate([k_ref[p_rows, :], k_cur], axis=0)
            vals = jnp.concatenate([v_ref[p_rows, :], v_cur], axis=0)
            bias = bias_ref[...]
        keys = keys.astype(BF16)
        vals = vals.astype(BF16)
        for kv in range(N_KV_HEADS):
            slabs = [q_ref[kv * HEADS_PER_SLAB + j, q_rows, :] for j in range(HEADS_PER_SLAB)]
            q4 = jnp.concatenate(
                [slabs[j][:, i * HEAD_DIM:(i + 1) * HEAD_DIM]
                 for j in range(HEADS_PER_SLAB) for i in range(HEADS_PER_SLAB)], axis=0)
            s = _dot_nt(q4.astype(BF16), keys[:, kv * HEAD_DIM:(kv + 1) * HEAD_DIM]) + bias
            m_new = jnp.max(s, axis=-1, keepdims=True)
            p = jnp.exp(s - m_new)
            l_new = jnp.sum(p, axis=-1, keepdims=True)
            pv = _dot(p.astype(BF16), vals[:, kv * HEAD_DIM:(kv + 1) * HEAD_DIM])
            for j in range(HEADS_PER_SLAB):
                slab = kv * HEADS_PER_SLAB + j
                r0 = (2 * j) * ATTN_BLOCK
                r1 = (2 * j + 1) * ATTN_BLOCK
                r2 = (2 * j + 2) * ATTN_BLOCK
                pv_s = jnp.concatenate([pv[r0:r1], pv[r1:r2]], axis=1)
                m_s = jnp.where(low_half, m_new[r0:r1], m_new[r1:r2])
                l_s = jnp.where(low_half, l_new[r0:r1], l_new[r1:r2])
                if init:
                    acc_ref[slab, q_rows, :] = pv_s
                    m_ref[slab, q_rows, :] = m_s
                    l_ref[slab, q_rows, :] = l_s
                else:
                    m_old = m_ref[slab, q_rows, :]
                    m_tot = jnp.maximum(m_old, m_s)
                    a_old = jnp.exp(m_old - m_tot)
                    a_new = jnp.exp(m_s - m_tot)
                    acc_ref[slab, q_rows, :] = acc_ref[slab, q_rows, :] * a_old + pv_s * a_new
                    l_ref[slab, q_rows, :] = l_ref[slab, q_rows, :] * a_old + l_s * a_new
                    m_ref[slab, q_rows, :] = m_tot

    for idx, (window, dilation) in enumerate(DILATED_PATTERNS):
        assert window // dilation == ATTN_BLOCK
        init = idx == 0
        n_blocks = seq // dilation // ATTN_BLOCK
        step = ATTN_BLOCK * dilation

        def residue(r, carry, dilation=dilation, init=init, n_blocks=n_blocks, step=step):
            block(r, None, dilation, init)

            def later(nb, c):
                start = r + nb * step
                if dilation == 1:
                    start = pl.multiple_of(start, ATTN_BLOCK)
                block(start, start - step, dilation, init)
                return c

            if n_blocks > 1:
                lax.fori_loop(1, n_blocks, later, 0)
            return carry

        if dilation == 1:
            residue(0, 0)
        else:
            lax.fori_loop(0, dilation, residue, 0)

    out_rows = 256

    def finish(i, c):
        r = pl.ds(pl.multiple_of(i * out_rows, out_rows), out_rows)
        for slab in range(Q_SLABS):
            o = acc_ref[slab, r, :] * (1.0 / l_ref[slab, r, :])
            o_ref[r, slab * LANES:(slab + 1) * LANES] = o.astype(o_ref.dtype)
        return c

    lax.fori_loop(0, seq // out_rows, finish, 0)


def _attn_bias():
    qi = np.arange(GQA_GROUP * ATTN_BLOCK)[:, None] % ATTN_BLOCK
    ki = np.arange(2 * ATTN_BLOCK)[None, :]
    valid = np.where(ki < ATTN_BLOCK, ki >= qi, ki - ATTN_BLOCK <= qi)
    return jnp.asarray(np.where(valid, 0.0, NEG_INF), F32)


def _attention(q, k, v, bsz, seq):
    n = bsz * seq
    return pl.pallas_call(
        functools.partial(_attn_kernel, seq=seq),
        grid=(bsz,),
        in_specs=[
            pl.BlockSpec((Q_SLABS, seq, LANES), lambda b: (0, b, 0)),
            pl.BlockSpec((seq, KV_WIDTH), lambda b: (b, 0)),
            pl.BlockSpec((seq, KV_WIDTH), lambda b: (b, 0)),
            pl.BlockSpec((GQA_GROUP * ATTN_BLOCK, 2 * ATTN_BLOCK), lambda b: (0, 0)),
        ],
        out_specs=pl.BlockSpec((seq, ATTN_WIDTH), lambda b: (b, 0)),
        out_shape=jax.ShapeDtypeStruct((n, ATTN_WIDTH), BF16),
        scratch_shapes=[pltpu.VMEM((Q_SLABS, seq, LANES), F32)] * 3,
        compiler_params=pltpu.CompilerParams(
            dimension_semantics=("parallel",), vmem_limit_bytes=VMEM_LIMIT_BYTES),
        name="attn",
    )(q, k, v, _attn_bias())


def _ssd_kernel(xbc_ref, z_ref, dt_ref, cw_ref, cb_ref, dtb_ref, aneg_ref, dskip_ref, nw_ref,
                tri_ref, sel_ref, expand_ref, lbias_ref,
                o_ref, pad_ref, u_ref, state_ref):
    j = pl.program_id(1)
    halo = SUBLANES

    @pl.when(j == 0)
    def _():
        pad_ref[0:halo, :] = jnp.zeros((halo, CONV_CH), F32)
        state_ref[...] = jnp.zeros(state_ref.shape, F32)

    pad_ref[halo:halo + SSD_ROWS, :] = xbc_ref[...]
    for c in range(SSD_ROWS // CHUNK):
        base = c * CHUNK + halo - (CONV_WIDTH - 1)
        u = cb_ref[...] + sum(
            pad_ref[base + t:base + t + CHUNK, :] * cw_ref[t:t + 1, :] for t in range(CONV_WIDTH))
        u_ref[c * CHUNK:(c + 1) * CHUNK, :] = _silu(u)
    pad_ref[0:halo, :] = pad_ref[SSD_ROWS:SSD_ROWS + halo, :]

    lane = lax.broadcasted_iota(jnp.int32, (CHUNK, LANES), 1)
    low_half = lane < SSM_HEAD_DIM
    lane_group = lane // SSM_HEADS
    tri = tri_ref[...]
    lbias = lbias_ref[...]

    def chunk(c, carry):
        r0 = pl.multiple_of(c * CHUNK, CHUNK)
        rows = pl.ds(r0, CHUNK)
        x = u_ref[rows, 0:SSM_INNER]
        bmat = u_ref[rows, SSM_INNER:SSM_INNER + BC_WIDTH]
        cmat = u_ref[rows, SSM_INNER + BC_WIDTH:CONV_CH]

        dt = _softplus(dt_ref[pl.ds(r0, CHUNK), :] + dtb_ref[...])
        a = dt * aneg_ref[...]
        cs = sum(_dot(tri, t) for t in _split3(a))
        cs3 = _split3(cs)
        dt3 = _split3(dt)
        terms = cs3 + dt3
        packed = jnp.zeros((CHUNK, LANES), BF16)
        for g, t in enumerate(terms):
            packed = jnp.where(lane_group == g, t, packed)
        spread = _dot(packed, expand_ref[...])
        cs_t3 = _dot_nt(sel_ref[...], packed)
        cs_t = cs_t3[0:SSM_HEADS] + cs_t3[SSM_HEADS:2 * SSM_HEADS] + cs_t3[2 * SSM_HEADS:3 * SSM_HEADS]

        bc_cols = SSM_HEADS * LANES
        cs_x = jnp.concatenate(
            [jnp.where(low_half, spread[:, (2 * i) * LANES:(2 * i + 1) * LANES],
                       spread[:, (2 * i + 1) * LANES:(2 * i + 2) * LANES])
             for i in range(SSM_HEADS // 2)], axis=1)
        dt_x = spread[:, bc_cols:bc_cols + SSM_INNER]
        last = cs_x[CHUNK - 1:CHUNK, :]
        xdt = x * dt_x
        xdt_b = xdt.astype(BF16)
        xw_b = (xdt * jnp.exp(last - cs_x)).astype(BF16)
        exp_cs = jnp.exp(cs_x)
        chunk_decay = jnp.exp(last)

        x_parts = []
        for g in range(SSM_GROUPS):
            b_g = bmat[:, g * D_STATE:(g + 1) * D_STATE]
            c_g = cmat[:, g * D_STATE:(g + 1) * D_STATE].astype(BF16)
            cb = _dot_nt(c_g, b_g.astype(BF16))
            gcols = slice(g * GROUP_WIDTH, (g + 1) * GROUP_WIDTH)
            state = state_ref[g]
            y_g = _dot(c_g, state.astype(BF16)) * exp_cs[:, gcols]
            state_ref[g] = state * chunk_decay[:, gcols] + _dot(b_g.T.astype(BF16), xw_b[:, gcols])
            diag = []
            for i in range(HEADS_PER_GROUP // 2):
                pair = g * (HEADS_PER_GROUP // 2) + i
                ms = []
                for e in (2 * pair, 2 * pair + 1):
                    diff = spread[:, e * LANES:(e + 1) * LANES] - cs_t[e:e + 1, :]
                    ms.append((cb * jnp.exp(diff + lbias)).astype(BF16))
                xs = xdt_b[:, pair * LANES:(pair + 1) * LANES]
                zero = jnp.zeros_like(xs)
                rhs = jnp.concatenate(
                    [jnp.where(low_half, xs, zero), jnp.where(low_half, zero, xs)], axis=0)
                diag.append(_dot(jnp.concatenate(ms, axis=1), rhs))
            x_parts.append(y_g + jnp.concatenate(diag, axis=1))
        y = jnp.concatenate(x_parts, axis=1) + dskip_ref[...] * x
        y = y * _silu(z_ref[pl.ds(r0, CHUNK), :])
        outs = []
        for g in range(SSM_GROUPS):
            y_g = y[:, g * GROUP_WIDTH:(g + 1) * GROUP_WIDTH]
            ms = jnp.mean(y_g * y_g, axis=-1, keepdims=True)
            outs.append(y_g * lax.rsqrt(ms + EPS))
        o = jnp.concatenate(outs, axis=1) * nw_ref[...]
        o_ref[pl.ds(r0, CHUNK), :] = o.astype(o_ref.dtype)
        return carry

    lax.fori_loop(0, SSD_ROWS // CHUNK, chunk, 0)


def _ssd_constants():
    tri = np.tril(np.ones((CHUNK, CHUNK), np.float32))
    expand = np.zeros((LANES, SSM_HEADS * LANES + SSM_INNER), np.float32)
    for lane in range(6 * SSM_HEADS):
        g, e = divmod(lane, SSM_HEADS)
        if g < 3:
            expand[lane, e * LANES:(e + 1) * LANES] = 1.0
        else:
            c0 = SSM_HEADS * LANES + e * SSM_HEAD_DIM
            expand[lane, c0:c0 + SSM_HEAD_DIM] = 1.0
    sel = np.eye(3 * SSM_HEADS, LANES, dtype=np.float32)
    lbias = np.where(np.tril(np.ones((CHUNK, CHUNK), bool)), 0.0, NEG_INF).astype(np.float32)
    return (jnp.asarray(tri, BF16), jnp.asarray(sel, BF16), jnp.asarray(expand, BF16),
            jnp.asarray(lbias, F32))


def _ssd(xbc, z, dt, conv_w, conv_b, dt_bias, a_neg, d_skip, norm_w, bsz, seq):
    n = bsz * seq
    steps = seq // SSD_ROWS
    row = lambda b, j: (b * steps + j, 0)
    const = lambda b, j: (0, 0)
    tri, sel, expand, lbias = _ssd_constants()

    def full(a):
        return pl.BlockSpec(a.shape, const)

    consts = (conv_w, conv_b, dt_bias, a_neg, d_skip, norm_w, tri, sel, expand, lbias)
    return pl.pallas_call(
        _ssd_kernel,
        grid=(bsz, steps),
        in_specs=[
            pl.BlockSpec((SSD_ROWS, CONV_CH), row),
            pl.BlockSpec((SSD_ROWS, SSM_INNER), row),
            pl.BlockSpec((SSD_ROWS, LANES), row),
        ] + [full(a) for a in consts],
        out_specs=pl.BlockSpec((SSD_ROWS, SSM_INNER), row),
        out_shape=jax.ShapeDtypeStruct((n, SSM_INNER), BF16),
        scratch_shapes=[
            pltpu.VMEM((SSD_ROWS + 2 * SUBLANES, CONV_CH), F32),
            pltpu.VMEM((SSD_ROWS, CONV_CH), F32),
            pltpu.VMEM((SSM_GROUPS, D_STATE, GROUP_WIDTH), F32),
        ],
        compiler_params=pltpu.CompilerParams(
            dimension_semantics=("parallel", "arbitrary"), vmem_limit_bytes=VMEM_LIMIT_BYTES),
        name="ssd",
    )(xbc, z, dt, *consts)


def _out_ffn_kernel(h_ref, attn_ref, y_ref, wo_ref, nw_ref, wg_ref, wu_ref, wd_ref, fw_ref,
                    o_ref, *, final_norm):
    h1 = (h_ref[...] + _dot(attn_ref[...], wo_ref[0:ATTN_WIDTH, :])
          + _dot(y_ref[...], wo_ref[ATTN_WIDTH:MIX_WIDTH, :]))
    ms = jnp.mean(h1 * h1, axis=-1, keepdims=True)
    hn = (h1 * lax.rsqrt(ms + EPS) * nw_ref[...]).astype(BF16)
    ffn = jnp.zeros_like(h1)
    for c in range(FFN_HIDDEN // FFN_CHUNK):
        cols = slice(c * FFN_CHUNK, (c + 1) * FFN_CHUNK)
        gate = _dot(hn, wg_ref[:, cols])
        up = _dot(hn, wu_ref[:, cols])
        ffn = ffn + _dot((_silu(gate) * up).astype(BF16), wd_ref[cols, :])
    acc = h1 + ffn
    if final_norm:
        ms = jnp.mean(acc * acc, axis=-1, keepdims=True)
        acc = acc * lax.rsqrt(ms + EPS) * fw_ref[...]
    o_ref[...] = acc


def _out_ffn(h, attn, y, w_out, norm_w, w_gate, w_up, w_down, final_w, final_norm):
    n = h.shape[0]
    tm = TOKEN_TILE
    row = lambda i: (i, 0)
    const = lambda i: (0, 0)

    def resident(a):
        return pl.BlockSpec(a.shape, const, pipeline_mode=pl.Buffered(1))

    return pl.pallas_call(
        functools.partial(_out_ffn_kernel, final_norm=final_norm),
        grid=(n // tm,),
        in_specs=[
            pl.BlockSpec((tm, D_MODEL), row),
            pl.BlockSpec((tm, ATTN_WIDTH), row),
            pl.BlockSpec((tm, SSM_INNER), row),
            resident(w_out), resident(norm_w), resident(w_gate), resident(w_up),
            resident(w_down), resident(final_w),
        ],
        out_specs=pl.BlockSpec((tm, D_MODEL), row),
        out_shape=jax.ShapeDtypeStruct((n, D_MODEL), F32),
        compiler_params=pltpu.CompilerParams(
            dimension_semantics=("parallel",), vmem_limit_bytes=VMEM_LIMIT_BYTES),
        name="out_ffn",
    )(h, attn, y, w_out, norm_w, w_gate, w_up, w_down, final_w)


def _rotary_tables(positions):
    inv_freq = ROPE_THETA ** (-jnp.arange(0, ROPE_DIM, 2, dtype=F32) / ROPE_DIM)
    ang = positions.astype(F32).reshape(-1, 1) * inv_freq
    cos, sin = jnp.cos(ang), jnp.sin(ang)
    n = ang.shape[0]
    pad = HEAD_DIM - ROPE_DIM
    ones = jnp.ones((n, pad), F32)
    zeros = jnp.zeros((n, pad), F32)
    z8 = jnp.zeros((n, ROPE_HALF), F32)
    rc = jnp.concatenate([cos, cos, ones], axis=1)
    rsa = jnp.concatenate([-sin, z8, zeros], axis=1)
    rsb = jnp.concatenate([z8, sin, zeros], axis=1)
    tile = lambda t: jnp.tile(t, (1, LANES // HEAD_DIM))
    return tile(rc), tile(rsa), tile(rsb)


def _lane_slab(v):
    return jnp.tile(v.astype(F32), DT_REPLICAS).reshape(1, LANES)


def kernel(x, positions, norm_mix, w_in, conv_w, conv_b, dt_bias, a_log, d_skip, ssm_norm,
           w_out, norm_ffn, w_gate, w_up, w_down, final_norm):
    bsz, seq, _ = x.shape
    depth = w_in.shape[0]
    n = bsz * seq
    assert seq % SSD_ROWS == 0 and n % TOKEN_TILE == 0
    assert all(seq % (ATTN_BLOCK * d) == 0 for _, d in DILATED_PATTERNS)
    rc, rsa, rsb = _rotary_tables(positions)
    h = x.reshape(n, D_MODEL)
    for layer in range(depth):
        w = w_in[layer]
        w_full = jnp.concatenate(
            [w[:, :XBC_END], jnp.tile(w[:, XBC_END:], (1, DT_REPLICAS))], axis=1).astype(BF16)
        q, k, v, z, xbc, dt = _in_proj(h, norm_mix[layer].reshape(1, D_MODEL), w_full, rc, rsa, rsb)
        attn = _attention(q, k, v, bsz, seq)
        y = _ssd(xbc, z, dt, conv_w[layer], conv_b[layer].reshape(1, CONV_CH),
                 _lane_slab(dt_bias[layer]), _lane_slab(-jnp.exp(a_log[layer].astype(F32))),
                 jnp.repeat(d_skip[layer].astype(F32), SSM_HEAD_DIM).reshape(1, SSM_INNER),
                 ssm_norm[layer].reshape(1, SSM_INNER), bsz, seq)
        h = _out_ffn(h, attn, y, w_out[layer].astype(BF16), norm_ffn[layer].reshape(1, D_MODEL),
                     w_gate[layer].astype(BF16), w_up[layer].astype(BF16),
                     w_down[layer].astype(BF16), final_norm.reshape(1, D_MODEL),
                     final_norm=(layer == depth - 1))
    return h.reshape(bsz, seq, D_MODEL)
```

```python
import functools
import math

import jax
import jax.numpy as jnp
import numpy as np
from jax import lax
from jax.experimental import pallas as pl
from jax.experimental.pallas import tpu as pltpu

F32 = jnp.float32
BF16 = jnp.bfloat16

D_MODEL = 1024
HEAD_DIM = 64
N_ATTN_HEADS = 8
N_KV_HEADS = 2
GQA_GROUP = N_ATTN_HEADS // N_KV_HEADS
ATTN_WIDTH = N_ATTN_HEADS * HEAD_DIM
ROPE_DIM = HEAD_DIM // 4
ROPE_HALF = ROPE_DIM // 2
ROPE_THETA = 500000.0
DILATED_PATTERNS = ((128, 1), (512, 4), (2048, 16))
ATTN_BLOCK = 128
SSM_HEAD_DIM = 64
SSM_HEADS = 16
SSM_INNER = SSM_HEADS * SSM_HEAD_DIM
SSM_GROUPS = 2
HEADS_PER_GROUP = SSM_HEADS // SSM_GROUPS
GROUP_WIDTH = SSM_INNER // SSM_GROUPS
D_STATE = 128
CONV_WIDTH = 4
CHUNK = 128
BC_WIDTH = SSM_GROUPS * D_STATE
CONV_CH = SSM_INNER + 2 * BC_WIDTH
MIX_WIDTH = ATTN_WIDTH + SSM_INNER
FFN_HIDDEN = 2816
EPS = 1e-5
LOG2E = math.log2(math.e)

LANES = 128
SUBLANES = 8
VMEM_LIMIT_BYTES = 56 * 1024 * 1024

Q_SLABS = ATTN_WIDTH // LANES
HEADS_PER_SLAB = LANES // HEAD_DIM
SLABS_PER_KV = GQA_GROUP // HEADS_PER_SLAB
XBC_SLABS = CONV_CH // LANES
X_SLABS = SSM_INNER // LANES
DT_REPLICAS = LANES // SSM_HEADS
NEG_INF = float("-inf")

COL_Q = 0
COL_K = COL_Q + ATTN_WIDTH
COL_V = COL_K + N_KV_HEADS * LANES
COL_Z = COL_V + N_KV_HEADS * LANES
COL_XBC = COL_Z + SSM_INNER
COL_DT = COL_XBC + CONV_CH
IN_PROJ_COLS = COL_DT + LANES

TOKEN_TILE = 512
SSD_ROWS = 512
FFN_CHUNK = 256
SORT_STRIDE = 4
BLOCKS_PER_TRIP = 4


def _dot(a, b):
    return jnp.dot(a, b, preferred_element_type=F32)


def _dot_nt(a, b):
    return lax.dot_general(a, b, (((1,), (1,)), ((), ())), preferred_element_type=F32)


def _split3(x):
    hi = x.astype(BF16)
    r1 = x - hi.astype(F32)
    mid = r1.astype(BF16)
    lo = (r1 - mid.astype(F32)).astype(BF16)
    return hi, mid, lo


def _silu(x):
    return x * (1.0 / (1.0 + jnp.exp(-x)))


def _softplus(x):
    return jnp.maximum(x, 0.0) + jnp.log(1.0 + jnp.exp(-jnp.abs(x)))


def _in_proj_kernel(x_ref, nw_ref, w_ref, rc_ref, rsa_ref, rsb_ref,
                    q_ref, k_ref, v_ref, z_ref, xbc_ref, dt_ref):
    x = x_ref[...]
    ms = jnp.mean(x * x, axis=-1, keepdims=True)
    hn = (x * lax.rsqrt(ms + EPS) * nw_ref[...]).astype(BF16)
    rc = rc_ref[...]
    rsa = rsa_ref[...]
    rsb = rsb_ref[...]

    def rope(t):
        return (t * rc + pltpu.roll(t, LANES - ROPE_HALF, axis=1) * rsa
                + pltpu.roll(t, ROPE_HALF, axis=1) * rsb)

    qk = _dot(hn, w_ref[:, COL_Q:COL_V])
    scale = HEAD_DIM ** -0.5 * LOG2E
    for s in range(Q_SLABS):
        q_ref[s] = rope(qk[:, s * LANES:(s + 1) * LANES]) * scale
    for h in range(N_KV_HEADS):
        k_ref[h] = rope(qk[:, COL_K + h * LANES:COL_K + (h + 1) * LANES])
    v = _dot(hn, w_ref[:, COL_V:COL_Z])
    for h in range(N_KV_HEADS):
        v_ref[h] = v[:, h * LANES:(h + 1) * LANES]
    z_ref[...] = _dot(hn, w_ref[:, COL_Z:COL_XBC])
    xbc = _dot(hn, w_ref[:, COL_XBC:COL_DT])
    for s in range(XBC_SLABS):
        xbc_ref[s] = xbc[:, s * LANES:(s + 1) * LANES]
    dt_ref[...] = _dot(hn, w_ref[:, COL_DT:IN_PROJ_COLS])


def _in_proj(h, norm_w, w, rc, rsa, rsb):
    n = h.shape[0]
    tm = TOKEN_TILE
    row = lambda i: (i, 0)
    slab = lambda i: (0, i, 0)
    const = lambda i: (0, 0)
    return pl.pallas_call(
        _in_proj_kernel,
        grid=(n // tm,),
        in_specs=[
            pl.BlockSpec((tm, D_MODEL), row),
            pl.BlockSpec((1, D_MODEL), const),
            pl.BlockSpec((D_MODEL, IN_PROJ_COLS), const),
            pl.BlockSpec((tm, LANES), row),
            pl.BlockSpec((tm, LANES), row),
            pl.BlockSpec((tm, LANES), row),
        ],
        out_specs=[
            pl.BlockSpec((Q_SLABS, tm, LANES), slab),
            pl.BlockSpec((N_KV_HEADS, tm, LANES), slab),
            pl.BlockSpec((N_KV_HEADS, tm, LANES), slab),
            pl.BlockSpec((tm, SSM_INNER), row),
            pl.BlockSpec((XBC_SLABS, tm, LANES), slab),
            pl.BlockSpec((tm, LANES), row),
        ],
        out_shape=[
            jax.ShapeDtypeStruct((Q_SLABS, n, LANES), F32),
            jax.ShapeDtypeStruct((N_KV_HEADS, n, LANES), F32),
            jax.ShapeDtypeStruct((N_KV_HEADS, n, LANES), F32),
            jax.ShapeDtypeStruct((n, SSM_INNER), F32),
            jax.ShapeDtypeStruct((XBC_SLABS, n, LANES), F32),
            jax.ShapeDtypeStruct((n, LANES), F32),
        ],
        compiler_params=pltpu.CompilerParams(
            dimension_semantics=("parallel",), vmem_limit_bytes=VMEM_LIMIT_BYTES),
        name="in_proj",
    )(h, norm_w, w, rc, rsa, rsb)


def _attn_kernel(q_ref, k_ref, v_ref, bias_ref, o_ref,
                 qs_ref, ks_ref, vs_ref, nat_ref, srt_ref, *, seq):
    lane = lax.broadcasted_iota(jnp.int32, (ATTN_BLOCK, LANES), 1)
    low = lane < HEAD_DIM
    sorted_rows = seq // SORT_STRIDE
    ACC, MAX, DEN = 0, 1, 2

    def ds(start, stride):
        if stride == 1:
            return pl.ds(start, ATTN_BLOCK)
        return pl.ds(start, ATTN_BLOCK, stride=stride)

    def block(q_src, k_src, v_src, q_rows, prev_rows, bias, load_old, store_new):
        k_cur = k_src[q_rows, :]
        v_cur = v_src[q_rows, :]
        if prev_rows is None:
            keys, vals = k_cur, v_cur
        else:
            keys = jnp.concatenate([k_src[prev_rows, :], k_cur], axis=0)
            vals = jnp.concatenate([v_src[prev_rows, :], v_cur], axis=0)
        lane_k = lax.broadcasted_iota(jnp.int32, vals.shape, 1) < HEAD_DIM
        vals_lo = jnp.where(lane_k, vals, 1.0).astype(BF16)
        vals_hi = jnp.where(lane_k, 1.0, vals).astype(BF16)
        slabs = [q_src[j, q_rows, :] for j in range(SLABS_PER_KV)]
        q4 = jnp.concatenate([jnp.where(low, t, 0.0) for t in slabs]
                             + [jnp.where(low, 0.0, t) for t in slabs], axis=0)
        s = _dot_nt(q4.astype(BF16), keys.astype(BF16)) + bias
        m_new = jnp.max(s, axis=-1, keepdims=True)
        p = jnp.exp2(s - m_new).astype(BF16)
        half = SLABS_PER_KV * ATTN_BLOCK
        pv_lo = _dot(p[:half], vals_lo)
        pv_hi = _dot(p[half:], vals_hi)
        for j in range(SLABS_PER_KV):
            r = slice(j * ATTN_BLOCK, (j + 1) * ATTN_BLOCK)
            r_hi = slice(half + j * ATTN_BLOCK, half + (j + 1) * ATTN_BLOCK)
            acc_n = jnp.where(low, pv_lo[r], pv_hi[r])
            den_n = pltpu.roll(jnp.where(low, pv_hi[r], pv_lo[r]), HEAD_DIM, axis=1)
            max_n = jnp.where(low, m_new[r], m_new[r_hi])
            if load_old is None:
                store_new(j, acc_n, max_n, den_n)
            else:
                acc_o, max_o, den_o = load_old(j)
                max_t = jnp.maximum(max_o, max_n)
                w_o = jnp.exp2(max_o - max_t)
                w_n = jnp.exp2(max_n - max_t)
                store_new(j, acc_o * w_o + acc_n * w_n, max_t, den_o * w_o + den_n * w_n)

    def aligned(start):
        return ds(pl.multiple_of(start, ATTN_BLOCK), 1)

    def windowed_bias(nb):
        if isinstance(nb, int):
            return bias_ref[min(nb, 1)]
        return bias_ref[jnp.minimum(nb, 1)]

    def prev_block(nb):
        return max(nb - 1, 0) if isinstance(nb, int) else jnp.maximum(nb - 1, 0)


    def branch1(nb):
        rows = aligned(nb * ATTN_BLOCK)

        def store(j, acc, mx, den):
            nat_ref[ACC, j, rows, :] = acc
            nat_ref[MAX, j, rows, :] = mx
            nat_ref[DEN, j, rows, :] = den

        block(q_ref, k_ref, v_ref, rows, aligned(prev_block(nb) * ATTN_BLOCK), windowed_bias(nb),
              None, store)

    def trip1(g, c):
        for i in range(BLOCKS_PER_TRIP):
            branch1(g * BLOCKS_PER_TRIP + i)
        return c

    lax.fori_loop(0, seq // ATTN_BLOCK // BLOCKS_PER_TRIP, trip1, 0)

    def sort_rows(i, c):
        r4 = i // (sorted_rows // ATTN_BLOCK)
        blk = i % (sorted_rows // ATTN_BLOCK)
        src = ds(r4 + blk * (ATTN_BLOCK * SORT_STRIDE), SORT_STRIDE)
        dst = aligned(i * ATTN_BLOCK)
        for j in range(SLABS_PER_KV):
            qs_ref[j, dst, :] = q_ref[j, src, :]
        ks_ref[dst, :] = k_ref[src, :]
        vs_ref[dst, :] = v_ref[src, :]
        return c

    lax.fori_loop(0, seq // ATTN_BLOCK, sort_rows, 0)

    def branch2(r4, nb):
        rows = aligned(r4 * sorted_rows + nb * ATTN_BLOCK)
        prev_rows = aligned(r4 * sorted_rows + prev_block(nb) * ATTN_BLOCK)
        nat_rows = ds(r4 + nb * (ATTN_BLOCK * SORT_STRIDE), SORT_STRIDE)

        def load(j):
            return tuple(nat_ref[t, j, nat_rows, :] for t in (ACC, MAX, DEN))

        def store(j, acc, mx, den):
            srt_ref[ACC, j, rows, :] = acc
            srt_ref[MAX, j, rows, :] = mx
            srt_ref[DEN, j, rows, :] = den

        block(qs_ref, ks_ref, vs_ref, rows, prev_rows, windowed_bias(nb), load, store)

    blocks_per_residue = sorted_rows // ATTN_BLOCK
    residues_per_trip = BLOCKS_PER_TRIP // blocks_per_residue

    def trip2(g, c):
        for i in range(residues_per_trip):
            for nb in range(blocks_per_residue):
                branch2(g * residues_per_trip + i, nb)
        return c

    lax.fori_loop(0, SORT_STRIDE // residues_per_trip, trip2, 0)

    def branch3(r16):
        start = (r16 % SORT_STRIDE) * sorted_rows + r16 // SORT_STRIDE
        rows = ds(start, SORT_STRIDE)
        out_rows = ds(r16, SORT_STRIDE * SORT_STRIDE)

        def load(j):
            return tuple(srt_ref[t, j, rows, :] for t in (ACC, MAX, DEN))

        def store(j, acc, mx, den):
            nat_ref[ACC, j, out_rows, :] = acc * (1.0 / den)

        block(qs_ref, ks_ref, vs_ref, rows, None, bias_ref[1, :, ATTN_BLOCK:], load, store)

    def trip3(g, c):
        for i in range(BLOCKS_PER_TRIP):
            branch3(g * BLOCKS_PER_TRIP + i)
        return c

    lax.fori_loop(0, SORT_STRIDE * SORT_STRIDE // BLOCKS_PER_TRIP, trip3, 0)

    out_chunk = 256

    def finish(i, c):
        r = pl.ds(pl.multiple_of(i * out_chunk, out_chunk), out_chunk)
        for j in range(SLABS_PER_KV):
            o_ref[r, j * LANES:(j + 1) * LANES] = nat_ref[ACC, j, r, :].astype(o_ref.dtype)
        return c

    lax.fori_loop(0, seq // out_chunk, finish, 0)


def _attn_bias():
    qi = np.arange(GQA_GROUP * ATTN_BLOCK)[:, None] % ATTN_BLOCK
    ki = np.arange(2 * ATTN_BLOCK)[None, :]
    own = (ki >= ATTN_BLOCK) & (ki - ATTN_BLOCK <= qi)
    valid = np.stack([own, own | ((ki < ATTN_BLOCK) & (ki >= qi))])
    return jnp.asarray(np.where(valid, 0.0, NEG_INF), F32)


def _attention(q, k, v, bsz, seq):
    n = bsz * seq
    sq = pl.Squeezed()
    return pl.pallas_call(
        functools.partial(_attn_kernel, seq=seq),
        grid=(bsz, N_KV_HEADS),
        in_specs=[
            pl.BlockSpec((SLABS_PER_KV, seq, LANES), lambda b, h: (h, b, 0)),
            pl.BlockSpec((sq, seq, LANES), lambda b, h: (h, b, 0)),
            pl.BlockSpec((sq, seq, LANES), lambda b, h: (h, b, 0)),
            pl.BlockSpec((2, GQA_GROUP * ATTN_BLOCK, 2 * ATTN_BLOCK), lambda b, h: (0, 0, 0)),
        ],
        out_specs=pl.BlockSpec((seq, SLABS_PER_KV * LANES), lambda b, h: (b, h)),
        out_shape=jax.ShapeDtypeStruct((n, ATTN_WIDTH), BF16),
        scratch_shapes=[
            pltpu.VMEM((SLABS_PER_KV, seq, LANES), F32),
            pltpu.VMEM((seq, LANES), F32),
            pltpu.VMEM((seq, LANES), F32),
            pltpu.VMEM((3, SLABS_PER_KV, seq, LANES), F32),
            pltpu.VMEM((3, SLABS_PER_KV, seq, LANES), F32),
        ],
        compiler_params=pltpu.CompilerParams(
            dimension_semantics=("parallel", "parallel"), vmem_limit_bytes=VMEM_LIMIT_BYTES),
        name="attn",
    )(q, k, v, _attn_bias())


def _ssd_kernel(xbc_ref, z_ref, dt_ref, cw_ref, cb_ref, dtb_ref, aneg_ref, dskip_ref, nw_ref,
                tri_ref, sel_ref, expand_ref, lbias_ref,
                o_ref, pad_ref, u_ref, state_ref):
    j = pl.program_id(1)
    halo = SUBLANES
    n_chunks = SSD_ROWS // CHUNK

    @pl.when(j == 0)
    def _():
        pad_ref[:, 0:halo, :] = jnp.zeros((XBC_SLABS, halo, LANES), F32)
        state_ref[...] = jnp.zeros(state_ref.shape, F32)

    pad_ref[:, halo:halo + SSD_ROWS, :] = xbc_ref[...]

    def conv(c, carry):
        base = c * CHUNK + halo - (CONV_WIDTH - 1)
        for s in range(XBC_SLABS):
            cols = slice(s * LANES, (s + 1) * LANES)
            taps = [pad_ref[s, pl.ds(base + t, CHUNK // 2, stride=2), :]
                    for t in range(CONV_WIDTH + 1)]
            for parity in range(2):
                u = cb_ref[:, cols] + sum(
                    taps[parity + t] * cw_ref[t:t + 1, cols] for t in range(CONV_WIDTH))
                u_ref[s, pl.ds(c * CHUNK + parity, CHUNK // 2, stride=2), :] = _silu(u)
        return carry

    lax.fori_loop(0, n_chunks, conv, 0)
    pad_ref[:, 0:halo, :] = pad_ref[:, SSD_ROWS:SSD_ROWS + halo, :]

    lane = lax.broadcasted_iota(jnp.int32, (CHUNK, LANES), 1)
    low_half = lane < SSM_HEAD_DIM
    lane_group = lane // SSM_HEADS
    tri = tri_ref[...]
    lbias = lbias_ref[...]

    def chunk(c, carry):
        r0 = pl.multiple_of(c * CHUNK, CHUNK)
        rows = pl.ds(r0, CHUNK)
        x = jnp.concatenate([u_ref[s, rows, :] for s in range(X_SLABS)], axis=1)
        b_slab = X_SLABS
        c_slab = X_SLABS + SSM_GROUPS

        dt = _softplus(dt_ref[rows, :] + dtb_ref[...])
        a = dt * aneg_ref[...]
        cs = sum(_dot(tri, t) for t in _split3(a))
        terms = _split3(cs) + _split3(dt)
        packed = jnp.zeros((CHUNK, LANES), BF16)
        for g, t in enumerate(terms):
            packed = jnp.where(lane_group == g, t, packed)
        spread = _dot(packed, expand_ref[...])
        cs_t3 = _dot_nt(sel_ref[...], packed)
        cs_t = cs_t3[0:SSM_HEADS] + cs_t3[SSM_HEADS:2 * SSM_HEADS] + cs_t3[2 * SSM_HEADS:3 * SSM_HEADS]

        bc_cols = SSM_HEADS * LANES
        cs_x = jnp.concatenate(
            [jnp.where(low_half, spread[:, (2 * i) * LANES:(2 * i + 1) * LANES],
                       spread[:, (2 * i + 1) * LANES:(2 * i + 2) * LANES])
             for i in range(SSM_HEADS // 2)], axis=1)
        dt_x = spread[:, bc_cols:bc_cols + SSM_INNER]
        last = cs_x[CHUNK - 1:CHUNK, :]
        xdt = x * dt_x
        xdt_b = xdt.astype(BF16)
        xw_b = (xdt * jnp.exp2(last - cs_x)).astype(BF16)
        exp_cs = jnp.exp2(cs_x)
        chunk_decay = jnp.exp2(last)

        y_parts = []
        for g in range(SSM_GROUPS):
            b_g = u_ref[b_slab + g, rows, :]
            c_g = u_ref[c_slab + g, rows, :].astype(BF16)
            cb = _dot_nt(c_g, b_g.astype(BF16))
            gcols = slice(g * GROUP_WIDTH, (g + 1) * GROUP_WIDTH)
            state = state_ref[g]
            y_g = _dot(c_g, state.astype(BF16)) * exp_cs[:, gcols]
            state_ref[g] = state * chunk_decay[:, gcols] + _dot(b_g.T.astype(BF16), xw_b[:, gcols])
            diag = []
            for i in range(HEADS_PER_GROUP // 2):
                pair = g * (HEADS_PER_GROUP // 2) + i
                ms = []
                for e in (2 * pair, 2 * pair + 1):
                    diff = spread[:, e * LANES:(e + 1) * LANES] - cs_t[e:e + 1, :]
                    ms.append((cb * jnp.exp2(diff + lbias)).astype(BF16))
                xs = xdt_b[:, pair * LANES:(pair + 1) * LANES]
                zero = jnp.zeros_like(xs)
                rhs = jnp.concatenate(
                    [jnp.where(low_half, xs, zero), jnp.where(low_half, zero, xs)], axis=0)
                diag.append(_dot(jnp.concatenate(ms, axis=1), rhs))
            y_parts.append(y_g + jnp.concatenate(diag, axis=1))
        y = jnp.concatenate(y_parts, axis=1) + dskip_ref[...] * x
        y = y * _silu(z_ref[rows, :])
        outs = []
        for g in range(SSM_GROUPS):
            y_g = y[:, g * GROUP_WIDTH:(g + 1) * GROUP_WIDTH]
            ms = jnp.mean(y_g * y_g, axis=-1, keepdims=True)
            outs.append(y_g * lax.rsqrt(ms + EPS))
        o = jnp.concatenate(outs, axis=1) * nw_ref[...]
        o_ref[rows, :] = o.astype(o_ref.dtype)
        return carry

    lax.fori_loop(0, n_chunks, chunk, 0)


def _ssd_constants():
    tri = np.tril(np.ones((CHUNK, CHUNK), np.float32))
    expand = np.zeros((LANES, SSM_HEADS * LANES + SSM_INNER), np.float32)
    for lane in range(6 * SSM_HEADS):
        g, e = divmod(lane, SSM_HEADS)
        if g < 3:
            expand[lane, e * LANES:(e + 1) * LANES] = 1.0
        else:
            c0 = SSM_HEADS * LANES + e * SSM_HEAD_DIM
            expand[lane, c0:c0 + SSM_HEAD_DIM] = 1.0
    sel = np.eye(3 * SSM_HEADS, LANES, dtype=np.float32)
    lbias = np.where(np.tril(np.ones((CHUNK, CHUNK), bool)), 0.0, NEG_INF).astype(np.float32)
    return (jnp.asarray(tri, BF16), jnp.asarray(sel, BF16), jnp.asarray(expand, BF16),
            jnp.asarray(lbias, F32))


def _ssd(xbc, z, dt, conv_w, conv_b, dt_bias, a_neg, d_skip, norm_w, bsz, seq):
    n = bsz * seq
    steps = seq // SSD_ROWS
    row = lambda b, j: (b * steps + j, 0)
    const = lambda b, j: (0, 0)
    tri, sel, expand, lbias = _ssd_constants()

    def full(a):
        return pl.BlockSpec(a.shape, const)

    consts = (conv_w, conv_b, dt_bias, a_neg, d_skip, norm_w, tri, sel, expand, lbias)
    return pl.pallas_call(
        _ssd_kernel,
        grid=(bsz, steps),
        in_specs=[
            pl.BlockSpec((XBC_SLABS, SSD_ROWS, LANES), lambda b, j: (0, b * steps + j, 0)),
            pl.BlockSpec((SSD_ROWS, SSM_INNER), row),
            pl.BlockSpec((SSD_ROWS, LANES), row),
        ] + [full(a) for a in consts],
        out_specs=pl.BlockSpec((SSD_ROWS, SSM_INNER), row),
        out_shape=jax.ShapeDtypeStruct((n, SSM_INNER), BF16),
        scratch_shapes=[
            pltpu.VMEM((XBC_SLABS, SSD_ROWS + 2 * SUBLANES, LANES), F32),
            pltpu.VMEM((XBC_SLABS, SSD_ROWS, LANES), F32),
            pltpu.VMEM((SSM_GROUPS, D_STATE, GROUP_WIDTH), F32),
        ],
        compiler_params=pltpu.CompilerParams(
            dimension_semantics=("parallel", "arbitrary"), vmem_limit_bytes=VMEM_LIMIT_BYTES),
        name="ssd",
    )(xbc, z, dt, *consts)


def _out_ffn_kernel(h_ref, attn_ref, y_ref, wo_ref, nw_ref, wg_ref, wu_ref, wd_ref, fw_ref,
                    o_ref, *, final_norm):
    h1 = (h_ref[...] + _dot(attn_ref[...], wo_ref[0:ATTN_WIDTH, :])
          + _dot(y_ref[...], wo_ref[ATTN_WIDTH:MIX_WIDTH, :]))
    ms = jnp.mean(h1 * h1, axis=-1, keepdims=True)
    hn = (h1 * lax.rsqrt(ms + EPS) * nw_ref[...]).astype(BF16)
    ffn = jnp.zeros_like(h1)
    for c in range(FFN_HIDDEN // FFN_CHUNK):
        cols = slice(c * FFN_CHUNK, (c + 1) * FFN_CHUNK)
        gate = _dot(hn, wg_ref[:, cols])
        up = _dot(hn, wu_ref[:, cols])
        ffn = ffn + _dot((_silu(gate) * up).astype(BF16), wd_ref[cols, :])
    acc = h1 + ffn
    if final_norm:
        ms = jnp.mean(acc * acc, axis=-1, keepdims=True)
        acc = acc * lax.rsqrt(ms + EPS) * fw_ref[...]
    o_ref[...] = acc


def _out_ffn(h, attn, y, w_out, norm_w, w_gate, w_up, w_down, final_w, final_norm):
    n = h.shape[0]
    tm = TOKEN_TILE
    row = lambda i: (i, 0)
    const = lambda i: (0, 0)

    def resident(a):
        return pl.BlockSpec(a.shape, const, pipeline_mode=pl.Buffered(1))

    return pl.pallas_call(
        functools.partial(_out_ffn_kernel, final_norm=final_norm),
        grid=(n // tm,),
        in_specs=[
            pl.BlockSpec((tm, D_MODEL), row),
            pl.BlockSpec((tm, ATTN_WIDTH), row),
            pl.BlockSpec((tm, SSM_INNER), row),
            resident(w_out), resident(norm_w), resident(w_gate), resident(w_up),
            resident(w_down), resident(final_w),
        ],
        out_specs=pl.BlockSpec((tm, D_MODEL), row),
        out_shape=jax.ShapeDtypeStruct((n, D_MODEL), F32),
        compiler_params=pltpu.CompilerParams(
            dimension_semantics=("parallel",), vmem_limit_bytes=VMEM_LIMIT_BYTES),
        name="out_ffn",
    )(h, attn, y, w_out, norm_w, w_gate, w_up, w_down, final_w)


def _rotary_tables(positions):
    inv_freq = ROPE_THETA ** (-jnp.arange(0, ROPE_DIM, 2, dtype=F32) / ROPE_DIM)
    ang = positions.astype(F32).reshape(-1, 1) * inv_freq
    cos, sin = jnp.cos(ang), jnp.sin(ang)
    n = ang.shape[0]
    pad = HEAD_DIM - ROPE_DIM
    ones = jnp.ones((n, pad), F32)
    zeros = jnp.zeros((n, pad), F32)
    z8 = jnp.zeros((n, ROPE_HALF), F32)
    rc = jnp.concatenate([cos, cos, ones], axis=1)
    rsa = jnp.concatenate([-sin, z8, zeros], axis=1)
    rsb = jnp.concatenate([z8, sin, zeros], axis=1)
    tile = lambda t: jnp.tile(t, (1, LANES // HEAD_DIM))
    return tile(rc), tile(rsa), tile(rsb)


def _lane_slab(v):
    return jnp.tile(v.astype(F32), DT_REPLICAS).reshape(1, LANES)


def _in_proj_weight(w):
    q_end = ATTN_WIDTH
    k_end = q_end + N_KV_HEADS * HEAD_DIM
    v_end = k_end + N_KV_HEADS * HEAD_DIM
    z_end = v_end + SSM_INNER
    xbc_end = z_end + CONV_CH

    def doubled(block):
        heads = [block[:, h * HEAD_DIM:(h + 1) * HEAD_DIM] for h in range(N_KV_HEADS)]
        return [t for h in heads for t in (h, h)]

    cols = ([w[:, :q_end]] + doubled(w[:, q_end:k_end]) + doubled(w[:, k_end:v_end])
            + [w[:, v_end:xbc_end], jnp.tile(w[:, xbc_end:], (1, DT_REPLICAS))])
    return jnp.concatenate(cols, axis=1).astype(BF16)


def kernel(x, positions, norm_mix, w_in, conv_w, conv_b, dt_bias, a_log, d_skip, ssm_norm,
           w_out, norm_ffn, w_gate, w_up, w_down, final_norm):
    bsz, seq, _ = x.shape
    depth = w_in.shape[0]
    n = bsz * seq
    assert seq % SSD_ROWS == 0 and n % TOKEN_TILE == 0
    assert DILATED_PATTERNS == ((128, 1), (512, 4), (2048, 16)) and seq % (16 * ATTN_BLOCK) == 0
    rc, rsa, rsb = _rotary_tables(positions)
    h = x.reshape(n, D_MODEL)
    for layer in range(depth):
        q, k, v, z, xbc, dt = _in_proj(h, norm_mix[layer].reshape(1, D_MODEL),
                                       _in_proj_weight(w_in[layer]), rc, rsa, rsb)
        attn = _attention(q, k, v, bsz, seq)
        a_neg = -jnp.exp(a_log[layer].astype(F32)) * LOG2E
        y = _ssd(xbc, z, dt, conv_w[layer], conv_b[layer].reshape(1, CONV_CH),
                 _lane_slab(dt_bias[layer]), _lane_slab(a_neg),
                 jnp.repeat(d_skip[layer].astype(F32), SSM_HEAD_DIM).reshape(1, SSM_INNER),
                 ssm_norm[layer].reshape(1, SSM_INNER), bsz, seq)
        h = _out_ffn(h, attn, y, w_out[layer].astype(BF16), norm_ffn[layer].reshape(1, D_MODEL),
                     w_gate[layer].astype(BF16), w_up[layer].astype(BF16),
                     w_down[layer].astype(BF16), final_norm.reshape(1, D_MODEL),
                     final_norm=(layer == depth - 1))
    return h.reshape(bsz, seq, D_MODEL)
```

```python
import functools
import math

import jax
import jax.numpy as jnp
import numpy as np
from jax import lax
from jax.experimental import pallas as pl
from jax.experimental.pallas import tpu as pltpu

F32 = jnp.float32
BF16 = jnp.bfloat16

D_MODEL = 1024
HEAD_DIM = 64
N_ATTN_HEADS = 8
N_KV_HEADS = 2
GQA_GROUP = N_ATTN_HEADS // N_KV_HEADS
ATTN_WIDTH = N_ATTN_HEADS * HEAD_DIM
ROPE_DIM = HEAD_DIM // 4
ROPE_HALF = ROPE_DIM // 2
ROPE_THETA = 500000.0
DILATED_PATTERNS = ((128, 1), (512, 4), (2048, 16))
ATTN_BLOCK = 128
SSM_HEAD_DIM = 64
SSM_HEADS = 16
SSM_INNER = SSM_HEADS * SSM_HEAD_DIM
SSM_GROUPS = 2
HEADS_PER_GROUP = SSM_HEADS // SSM_GROUPS
GROUP_WIDTH = SSM_INNER // SSM_GROUPS
D_STATE = 128
CONV_WIDTH = 4
CHUNK = 128
BC_WIDTH = SSM_GROUPS * D_STATE
CONV_CH = SSM_INNER + 2 * BC_WIDTH
MIX_WIDTH = ATTN_WIDTH + SSM_INNER
FFN_HIDDEN = 2816
EPS = 1e-5
LOG2E = math.log2(math.e)

LANES = 128
SUBLANES = 8
VMEM_LIMIT_BYTES = 56 * 1024 * 1024

Q_SLABS = ATTN_WIDTH // LANES
HEADS_PER_SLAB = LANES // HEAD_DIM
SLABS_PER_KV = GQA_GROUP // HEADS_PER_SLAB
XBC_SLABS = CONV_CH // LANES
X_SLABS = SSM_INNER // LANES
DT_REPLICAS = LANES // SSM_HEADS
NEG_INF = float("-inf")

COL_Q = 0
COL_K = COL_Q + ATTN_WIDTH
COL_V = COL_K + N_KV_HEADS * LANES
COL_Z = COL_V + N_KV_HEADS * LANES
COL_XBC = COL_Z + SSM_INNER
COL_DT = COL_XBC + CONV_CH
IN_PROJ_COLS = COL_DT + LANES

TOKEN_TILE = 512
SSD_ROWS = 512
FFN_CHUNK = 256
SORT_STRIDE = 4
BLOCKS_PER_TRIP = 4
SSD_CHUNKS_PER_TRIP = 2


def _dot(a, b):
    return jnp.dot(a, b, preferred_element_type=F32)


def _dot_nt(a, b):
    return lax.dot_general(a, b, (((1,), (1,)), ((), ())), preferred_element_type=F32)


def _split3(x):
    hi = x.astype(BF16)
    r1 = x - hi.astype(F32)
    mid = r1.astype(BF16)
    lo = (r1 - mid.astype(F32)).astype(BF16)
    return hi, mid, lo


def _silu(x):
    return x * (1.0 / (1.0 + jnp.exp(-x)))


def _softplus(x):
    return jnp.maximum(x, 0.0) + jnp.log(1.0 + jnp.exp(-jnp.abs(x)))


def _in_proj_kernel(x_ref, nw_ref, w_ref, rc_ref, rsa_ref, rsb_ref,
                    q_ref, k_ref, v_ref, z_ref, xbc_ref, dt_ref):
    x = x_ref[...]
    ms = jnp.mean(x * x, axis=-1, keepdims=True)
    hn = (x * lax.rsqrt(ms + EPS) * nw_ref[...]).astype(BF16)
    rc = rc_ref[...]
    rsa = rsa_ref[...]
    rsb = rsb_ref[...]

    def rope(t):
        return (t * rc + pltpu.roll(t, LANES - ROPE_HALF, axis=1) * rsa
                + pltpu.roll(t, ROPE_HALF, axis=1) * rsb)

    qk = _dot(hn, w_ref[:, COL_Q:COL_V])
    scale = HEAD_DIM ** -0.5 * LOG2E
    for s in range(Q_SLABS):
        q_ref[s] = rope(qk[:, s * LANES:(s + 1) * LANES]) * scale
    for h in range(N_KV_HEADS):
        k_ref[h] = rope(qk[:, COL_K + h * LANES:COL_K + (h + 1) * LANES])
    v = _dot(hn, w_ref[:, COL_V:COL_Z])
    for h in range(N_KV_HEADS):
        v_ref[h] = v[:, h * LANES:(h + 1) * LANES]
    z_ref[...] = _dot(hn, w_ref[:, COL_Z:COL_XBC])
    xbc = _dot(hn, w_ref[:, COL_XBC:COL_DT])
    for s in range(XBC_SLABS):
        xbc_ref[s] = xbc[:, s * LANES:(s + 1) * LANES]
    dt_ref[...] = _dot(hn, w_ref[:, COL_DT:IN_PROJ_COLS])


def _in_proj(h, norm_w, w, rc, rsa, rsb):
    n = h.shape[0]
    tm = TOKEN_TILE
    row = lambda i: (i, 0)
    slab = lambda i: (0, i, 0)
    const = lambda i: (0, 0)
    return pl.pallas_call(
        _in_proj_kernel,
        grid=(n // tm,),
        in_specs=[
            pl.BlockSpec((tm, D_MODEL), row),
            pl.BlockSpec((1, D_MODEL), const),
            pl.BlockSpec((D_MODEL, IN_PROJ_COLS), const),
            pl.BlockSpec((tm, LANES), row),
            pl.BlockSpec((tm, LANES), row),
            pl.BlockSpec((tm, LANES), row),
        ],
        out_specs=[
            pl.BlockSpec((Q_SLABS, tm, LANES), slab),
            pl.BlockSpec((N_KV_HEADS, tm, LANES), slab),
            pl.BlockSpec((N_KV_HEADS, tm, LANES), slab),
            pl.BlockSpec((tm, SSM_INNER), row),
            pl.BlockSpec((XBC_SLABS, tm, LANES), slab),
            pl.BlockSpec((tm, LANES), row),
        ],
        out_shape=[
            jax.ShapeDtypeStruct((Q_SLABS, n, LANES), F32),
            jax.ShapeDtypeStruct((N_KV_HEADS, n, LANES), F32),
            jax.ShapeDtypeStruct((N_KV_HEADS, n, LANES), F32),
            jax.ShapeDtypeStruct((n, SSM_INNER), F32),
            jax.ShapeDtypeStruct((XBC_SLABS, n, LANES), F32),
            jax.ShapeDtypeStruct((n, LANES), F32),
        ],
        compiler_params=pltpu.CompilerParams(
            dimension_semantics=("parallel",), vmem_limit_bytes=VMEM_LIMIT_BYTES),
        name="in_proj",
    )(h, norm_w, w, rc, rsa, rsb)


def _attn_kernel(q_ref, k_ref, v_ref, bias_ref, o_ref,
                 qs_ref, ks_ref, vs_ref, nat_ref, srt_ref, p_ref, mx_ref, *, seq):
    lane = lax.broadcasted_iota(jnp.int32, (ATTN_BLOCK, LANES), 1)
    low = lane < HEAD_DIM
    sorted_rows = seq // SORT_STRIDE
    half = SLABS_PER_KV * ATTN_BLOCK
    ACC, MAX, DEN = 0, 1, 2

    def ds(start, stride):
        if stride == 1:
            return pl.ds(start, ATTN_BLOCK)
        return pl.ds(start, ATTN_BLOCK, stride=stride)

    def aligned(start):
        return ds(pl.multiple_of(start, ATTN_BLOCK), 1)

    def gather(src, q_rows, prev_rows):
        cur = src[q_rows, :]
        if prev_rows is None:
            return cur
        return jnp.concatenate([src[prev_rows, :], cur], axis=0)

    def score_stage(slot, i, q_src, k_src, q_rows, prev_rows, bias):
        keys = gather(k_src, q_rows, prev_rows)
        slabs = [q_src[j, q_rows, :] for j in range(SLABS_PER_KV)]
        q4 = jnp.concatenate([jnp.where(low, t, 0.0) for t in slabs]
                             + [jnp.where(low, 0.0, t) for t in slabs], axis=0)
        s = _dot_nt(q4.astype(BF16), keys.astype(BF16)) + bias
        m_new = jnp.max(s, axis=-1, keepdims=True)
        p_ref[slot, i, :, 0:keys.shape[0]] = jnp.exp2(s - m_new).astype(BF16)
        for j in range(SLABS_PER_KV):
            lo = m_new[j * ATTN_BLOCK:(j + 1) * ATTN_BLOCK]
            hi = m_new[half + j * ATTN_BLOCK:half + (j + 1) * ATTN_BLOCK]
            mx_ref[slot, i, j] = jnp.where(low, lo, hi)

    def value_stage(slot, i, v_src, q_rows, prev_rows, load_old, store_new):
        vals = gather(v_src, q_rows, prev_rows)
        lane_k = lax.broadcasted_iota(jnp.int32, vals.shape, 1) < HEAD_DIM
        vals_lo = jnp.where(lane_k, vals, 1.0).astype(BF16)
        vals_hi = jnp.where(lane_k, 1.0, vals).astype(BF16)
        nk = vals.shape[0]
        pv_lo = _dot(p_ref[slot, i, 0:half, 0:nk], vals_lo)
        pv_hi = _dot(p_ref[slot, i, half:2 * half, 0:nk], vals_hi)
        for j in range(SLABS_PER_KV):
            r = slice(j * ATTN_BLOCK, (j + 1) * ATTN_BLOCK)
            acc_n = jnp.where(low, pv_lo[r], pv_hi[r])
            den_n = pltpu.roll(jnp.where(low, pv_hi[r], pv_lo[r]), HEAD_DIM, axis=1)
            max_n = mx_ref[slot, i, j]
            if load_old is None:
                store_new(j, acc_n, max_n, den_n)
            else:
                acc_o, max_o, den_o = load_old(j)
                max_t = jnp.maximum(max_o, max_n)
                w_o = jnp.exp2(max_o - max_t)
                w_n = jnp.exp2(max_n - max_t)
                store_new(j, acc_o * w_o + acc_n * w_n, max_t, den_o * w_o + den_n * w_n)

    def windowed_bias(nb):
        if isinstance(nb, int):
            return bias_ref[min(nb, 1)]
        return bias_ref[jnp.minimum(nb, 1)]

    def prev_block(nb):
        return max(nb - 1, 0) if isinstance(nb, int) else jnp.maximum(nb - 1, 0)

    def rows1(g, i):
        nb = g * BLOCKS_PER_TRIP + i
        return nb, aligned(nb * ATTN_BLOCK), aligned(prev_block(nb) * ATTN_BLOCK)

    def score1(g, slot):
        for i in range(BLOCKS_PER_TRIP):
            nb, rows, prev_rows = rows1(g, i)
            score_stage(slot, i, q_ref, k_ref, rows, prev_rows, windowed_bias(nb))

    def value1(g, slot):
        for i in range(BLOCKS_PER_TRIP):
            _, rows, prev_rows = rows1(g, i)

            def store(j, acc, mx, den, rows=rows):
                nat_ref[ACC, j, rows, :] = acc
                nat_ref[MAX, j, rows, :] = mx
                nat_ref[DEN, j, rows, :] = den

            value_stage(slot, i, v_ref, rows, prev_rows, None, store)

    assert sorted_rows // ATTN_BLOCK == BLOCKS_PER_TRIP

    def rows2(r4, nb):
        return (aligned(r4 * sorted_rows + nb * ATTN_BLOCK),
                aligned(r4 * sorted_rows + prev_block(nb) * ATTN_BLOCK))

    def score2(r4, slot):
        for nb in range(BLOCKS_PER_TRIP):
            rows, prev_rows = rows2(r4, nb)
            score_stage(slot, nb, qs_ref, ks_ref, rows, prev_rows, windowed_bias(nb))

    def value2(r4, slot):
        for nb in range(BLOCKS_PER_TRIP):
            rows, prev_rows = rows2(r4, nb)
            nat_rows = ds(r4 + nb * (ATTN_BLOCK * SORT_STRIDE), SORT_STRIDE)

            def load(j, nat_rows=nat_rows):
                return tuple(nat_ref[t, j, nat_rows, :] for t in (ACC, MAX, DEN))

            def store(j, acc, mx, den, rows=rows):
                srt_ref[ACC, j, rows, :] = acc
                srt_ref[MAX, j, rows, :] = mx
                srt_ref[DEN, j, rows, :] = den

            value_stage(slot, nb, vs_ref, rows, prev_rows, load, store)

    def rows3(g, i):
        r16 = g * BLOCKS_PER_TRIP + i
        start = (r16 % SORT_STRIDE) * sorted_rows + r16 // SORT_STRIDE
        return r16, ds(start, SORT_STRIDE)

    def score3(g, slot):
        for i in range(BLOCKS_PER_TRIP):
            _, rows = rows3(g, i)
            score_stage(slot, i, qs_ref, ks_ref, rows, None, bias_ref[1, :, ATTN_BLOCK:])

    def value3(g, slot):
        for i in range(BLOCKS_PER_TRIP):
            r16, rows = rows3(g, i)
            out_rows = ds(r16, SORT_STRIDE * SORT_STRIDE)

            def load(j, rows=rows):
                return tuple(srt_ref[t, j, rows, :] for t in (ACC, MAX, DEN))

            def store(j, acc, mx, den, out_rows=out_rows):
                nat_ref[ACC, j, out_rows, :] = acc * (1.0 / den)

            value_stage(slot, i, vs_ref, rows, None, load, store)

    def sort_rows(i, c):
        r4 = i // (sorted_rows // ATTN_BLOCK)
        blk = i % (sorted_rows // ATTN_BLOCK)
        src = ds(r4 + blk * (ATTN_BLOCK * SORT_STRIDE), SORT_STRIDE)
        dst = aligned(i * ATTN_BLOCK)
        for j in range(SLABS_PER_KV):
            qs_ref[j, dst, :] = q_ref[j, src, :]
        ks_ref[dst, :] = k_ref[src, :]
        vs_ref[dst, :] = v_ref[src, :]
        return c

    lax.fori_loop(0, seq // ATTN_BLOCK, sort_rows, 0)

    branches = ((score1, value1, seq // ATTN_BLOCK // BLOCKS_PER_TRIP),
                (score2, value2, SORT_STRIDE),
                (score3, value3, SORT_STRIDE * SORT_STRIDE // BLOCKS_PER_TRIP))
    assert all(n % 2 == 0 for _, _, n in branches)
    branches[0][0](0, 0)
    for idx, (score, value, n_groups) in enumerate(branches):
        def body(g, c, score=score, value=value):
            value(g, g % 2)
            score(g + 1, (g + 1) % 2)
            return c

        lax.fori_loop(0, n_groups - 1, body, 0)
        value(n_groups - 1, 1)
        if idx + 1 < len(branches):
            branches[idx + 1][0](0, 0)

    out_chunk = 256

    def finish(i, c):
        r = pl.ds(pl.multiple_of(i * out_chunk, out_chunk), out_chunk)
        for j in range(SLABS_PER_KV):
            o_ref[r, j * LANES:(j + 1) * LANES] = nat_ref[ACC, j, r, :].astype(o_ref.dtype)
        return c

    lax.fori_loop(0, seq // out_chunk, finish, 0)


def _attn_bias():
    qi = np.arange(GQA_GROUP * ATTN_BLOCK)[:, None] % ATTN_BLOCK
    ki = np.arange(2 * ATTN_BLOCK)[None, :]
    own = (ki >= ATTN_BLOCK) & (ki - ATTN_BLOCK <= qi)
    valid = np.stack([own, own | ((ki < ATTN_BLOCK) & (ki >= qi))])
    return jnp.asarray(np.where(valid, 0.0, NEG_INF), F32)


def _attention(q, k, v, bsz, seq):
    n = bsz * seq
    sq = pl.Squeezed()
    return pl.pallas_call(
        functools.partial(_attn_kernel, seq=seq),
        grid=(bsz, N_KV_HEADS),
        in_specs=[
            pl.BlockSpec((SLABS_PER_KV, seq, LANES), lambda b, h: (h, b, 0)),
            pl.BlockSpec((sq, seq, LANES), lambda b, h: (h, b, 0)),
            pl.BlockSpec((sq, seq, LANES), lambda b, h: (h, b, 0)),
            pl.BlockSpec((2, GQA_GROUP * ATTN_BLOCK, 2 * ATTN_BLOCK), lambda b, h: (0, 0, 0)),
        ],
        out_specs=pl.BlockSpec((seq, SLABS_PER_KV * LANES), lambda b, h: (b, h)),
        out_shape=jax.ShapeDtypeStruct((n, ATTN_WIDTH), BF16),
        scratch_shapes=[
            pltpu.VMEM((SLABS_PER_KV, seq, LANES), F32),
            pltpu.VMEM((seq, LANES), F32),
            pltpu.VMEM((seq, LANES), F32),
            pltpu.VMEM((3, SLABS_PER_KV, seq, LANES), F32),
            pltpu.VMEM((3, SLABS_PER_KV, seq, LANES), F32),
            pltpu.VMEM((2, BLOCKS_PER_TRIP, GQA_GROUP * ATTN_BLOCK, 2 * ATTN_BLOCK), BF16),
            pltpu.VMEM((2, BLOCKS_PER_TRIP, SLABS_PER_KV, ATTN_BLOCK, LANES), F32),
        ],
        compiler_params=pltpu.CompilerParams(
            dimension_semantics=("parallel", "parallel"), vmem_limit_bytes=VMEM_LIMIT_BYTES),
        name="attn",
    )(q, k, v, _attn_bias())


def _ssd_kernel(xbc_ref, z_ref, dt_ref, cw_ref, cb_ref, dtb_ref, aneg_ref, dskip_ref, nw_ref,
                tri_ref, sel_ref, expand_ref, lbias_ref,
                o_ref, pad_ref, u_ref, state_ref, spread_ref, cst_ref, *, seq):
    j = pl.program_id(1)
    halo = SUBLANES
    n_chunks = SSD_ROWS // CHUNK

    @pl.when(j == 0)
    def _():
        pad_ref[:, 0:halo, :] = jnp.zeros((XBC_SLABS, halo, LANES), F32)
        state_ref[...] = jnp.zeros(state_ref.shape, F32)

    pad_ref[:, halo:halo + SSD_ROWS, :] = xbc_ref[...]

    def conv(c, carry):
        base = c * CHUNK + halo - (CONV_WIDTH - 1)
        for s in range(XBC_SLABS):
            cols = slice(s * LANES, (s + 1) * LANES)
            taps = [pad_ref[s, pl.ds(base + t, CHUNK // 2, stride=2), :]
                    for t in range(CONV_WIDTH + 1)]
            for parity in range(2):
                u = cb_ref[:, cols] + sum(
                    taps[parity + t] * cw_ref[t:t + 1, cols] for t in range(CONV_WIDTH))
                u_ref[s, pl.ds(c * CHUNK + parity, CHUNK // 2, stride=2), :] = _silu(u)
        return carry

    lax.fori_loop(0, n_chunks, conv, 0)
    pad_ref[:, 0:halo, :] = pad_ref[:, SSD_ROWS:SSD_ROWS + halo, :]

    lane = lax.broadcasted_iota(jnp.int32, (CHUNK, LANES), 1)
    low_half = lane < SSM_HEAD_DIM
    lane_group = lane // SSM_HEADS
    tri = tri_ref[...]
    lbias = lbias_ref[...]

    chunks_per_seq = seq // CHUNK

    def decay_stage(cg, slot):
        rows = pl.ds(pl.multiple_of(cg * CHUNK, CHUNK), CHUNK)
        dt = _softplus(dt_ref[rows, :] + dtb_ref[...])
        a = dt * aneg_ref[...]
        cs = sum(_dot(tri, t) for t in _split3(a))
        terms = _split3(cs) + _split3(dt)
        packed = jnp.zeros((CHUNK, LANES), BF16)
        for g, t in enumerate(terms):
            packed = jnp.where(lane_group == g, t, packed)
        spread_ref[slot] = _dot(packed, expand_ref[...])
        cs_t3 = _dot_nt(sel_ref[...], packed)
        cst_ref[slot] = (cs_t3[0:SSM_HEADS] + cs_t3[SSM_HEADS:2 * SSM_HEADS]
                         + cs_t3[2 * SSM_HEADS:3 * SSM_HEADS])

    def scan_stage(c, slot):
        rows = pl.ds(pl.multiple_of(c * CHUNK, CHUNK), CHUNK)
        x = jnp.concatenate([u_ref[s, rows, :] for s in range(X_SLABS)], axis=1)
        b_slab = X_SLABS
        c_slab = X_SLABS + SSM_GROUPS

        def spread(e):
            return spread_ref[slot, :, e * LANES:(e + 1) * LANES]

        cs_t = cst_ref[slot]
        bc_cols = SSM_HEADS * LANES
        cs_x = jnp.concatenate(
            [jnp.where(low_half, spread(2 * i), spread(2 * i + 1))
             for i in range(SSM_HEADS // 2)], axis=1)
        dt_x = spread_ref[slot, :, bc_cols:bc_cols + SSM_INNER]
        last = cs_x[CHUNK - 1:CHUNK, :]
        xdt = x * dt_x
        xdt_b = xdt.astype(BF16)
        xw_b = (xdt * jnp.exp2(last - cs_x)).astype(BF16)
        exp_cs = jnp.exp2(cs_x)
        chunk_decay = jnp.exp2(last)

        y_parts = []
        for g in range(SSM_GROUPS):
            b_g = u_ref[b_slab + g, rows, :]
            c_g = u_ref[c_slab + g, rows, :].astype(BF16)
            cb = _dot_nt(c_g, b_g.astype(BF16))
            gcols = slice(g * GROUP_WIDTH, (g + 1) * GROUP_WIDTH)
            state = state_ref[g]
            y_g = _dot(c_g, state.astype(BF16)) * exp_cs[:, gcols]
            state_ref[g] = state * chunk_decay[:, gcols] + _dot(b_g.T.astype(BF16), xw_b[:, gcols])
            diag = []
            for i in range(HEADS_PER_GROUP // 2):
                pair = g * (HEADS_PER_GROUP // 2) + i
                ms = []
                for e in (2 * pair, 2 * pair + 1):
                    diff = spread(e) - cs_t[e:e + 1, :]
                    ms.append((cb * jnp.exp2(diff + lbias)).astype(BF16))
                xs = xdt_b[:, pair * LANES:(pair + 1) * LANES]
                zero = jnp.zeros_like(xs)
                rhs = jnp.concatenate(
                    [jnp.where(low_half, xs, zero), jnp.where(low_half, zero, xs)], axis=0)
                diag.append(_dot(jnp.concatenate(ms, axis=1), rhs))
            y_parts.append(y_g + jnp.concatenate(diag, axis=1))
        y = jnp.concatenate(y_parts, axis=1) + dskip_ref[...] * x
        y = y * _silu(z_ref[rows, :])
        outs = []
        for g in range(SSM_GROUPS):
            y_g = y[:, g * GROUP_WIDTH:(g + 1) * GROUP_WIDTH]
            ms = jnp.mean(y_g * y_g, axis=-1, keepdims=True)
            outs.append(y_g * lax.rsqrt(ms + EPS))
        o = jnp.concatenate(outs, axis=1) * nw_ref[...]
        o_ref[rows, :] = o.astype(o_ref.dtype)

    @pl.when(j == 0)
    def _():
        decay_stage(0, 0)

    assert SSD_CHUNKS_PER_TRIP % 2 == 0 and n_chunks % SSD_CHUNKS_PER_TRIP == 0

    def chunk_group(g, carry):
        for i in range(SSD_CHUNKS_PER_TRIP):
            c = g * SSD_CHUNKS_PER_TRIP + i
            cg = j * n_chunks + c
            scan_stage(c, i % 2)
            decay_stage(jnp.minimum(cg + 1, chunks_per_seq - 1), (i + 1) % 2)
        return carry

    lax.fori_loop(0, n_chunks // SSD_CHUNKS_PER_TRIP, chunk_group, 0)


def _ssd_constants():
    tri = np.tril(np.ones((CHUNK, CHUNK), np.float32))
    expand = np.zeros((LANES, SSM_HEADS * LANES + SSM_INNER), np.float32)
    for lane in range(6 * SSM_HEADS):
        g, e = divmod(lane, SSM_HEADS)
        if g < 3:
            expand[lane, e * LANES:(e + 1) * LANES] = 1.0
        else:
            c0 = SSM_HEADS * LANES + e * SSM_HEAD_DIM
            expand[lane, c0:c0 + SSM_HEAD_DIM] = 1.0
    sel = np.eye(3 * SSM_HEADS, LANES, dtype=np.float32)
    lbias = np.where(np.tril(np.ones((CHUNK, CHUNK), bool)), 0.0, NEG_INF).astype(np.float32)
    return (jnp.asarray(tri, BF16), jnp.asarray(sel, BF16), jnp.asarray(expand, BF16),
            jnp.asarray(lbias, F32))


def _ssd(xbc, z, dt, conv_w, conv_b, dt_bias, a_neg, d_skip, norm_w, bsz, seq):
    n = bsz * seq
    steps = seq // SSD_ROWS
    row = lambda b, j: (b * steps + j, 0)
    const = lambda b, j: (0, 0)
    tri, sel, expand, lbias = _ssd_constants()

    def full(a):
        return pl.BlockSpec(a.shape, const)

    consts = (conv_w, conv_b, dt_bias, a_neg, d_skip, norm_w, tri, sel, expand, lbias)
    return pl.pallas_call(
        functools.partial(_ssd_kernel, seq=seq),
        grid=(bsz, steps),
        in_specs=[
            pl.BlockSpec((XBC_SLABS, SSD_ROWS, LANES), lambda b, j: (0, b * steps + j, 0)),
            pl.BlockSpec((SSD_ROWS, SSM_INNER), row),
            pl.BlockSpec((seq, LANES), lambda b, j: (b, 0)),
        ] + [full(a) for a in consts],
        out_specs=pl.BlockSpec((SSD_ROWS, SSM_INNER), row),
        out_shape=jax.ShapeDtypeStruct((n, SSM_INNER), BF16),
        scratch_shapes=[
            pltpu.VMEM((XBC_SLABS, SSD_ROWS + 2 * SUBLANES, LANES), F32),
            pltpu.VMEM((XBC_SLABS, SSD_ROWS, LANES), F32),
            pltpu.VMEM((SSM_GROUPS, D_STATE, GROUP_WIDTH), F32),
            pltpu.VMEM((2, CHUNK, SSM_HEADS * LANES + SSM_INNER), F32),
            pltpu.VMEM((2, SSM_HEADS, LANES), F32),
        ],
        compiler_params=pltpu.CompilerParams(
            dimension_semantics=("parallel", "arbitrary"), vmem_limit_bytes=VMEM_LIMIT_BYTES),
        name="ssd",
    )(xbc, z, dt, *consts)


def _out_ffn_kernel(h_ref, attn_ref, y_ref, wo_ref, nw_ref, wg_ref, wu_ref, wd_ref, fw_ref,
                    o_ref, *, final_norm):
    h1 = (h_ref[...] + _dot(attn_ref[...], wo_ref[0:ATTN_WIDTH, :])
          + _dot(y_ref[...], wo_ref[ATTN_WIDTH:MIX_WIDTH, :]))
    ms = jnp.mean(h1 * h1, axis=-1, keepdims=True)
    hn = (h1 * lax.rsqrt(ms + EPS) * nw_ref[...]).astype(BF16)
    ffn = jnp.zeros_like(h1)
    for c in range(FFN_HIDDEN // FFN_CHUNK):
        cols = slice(c * FFN_CHUNK, (c + 1) * FFN_CHUNK)
        gate = _dot(hn, wg_ref[:, cols])
        up = _dot(hn, wu_ref[:, cols])
        ffn = ffn + _dot((_silu(gate) * up).astype(BF16), wd_ref[cols, :])
    acc = h1 + ffn
    if final_norm:
        ms = jnp.mean(acc * acc, axis=-1, keepdims=True)
        acc = acc * lax.rsqrt(ms + EPS) * fw_ref[...]
    o_ref[...] = acc


def _out_ffn(h, attn, y, w_out, norm_w, w_gate, w_up, w_down, final_w, final_norm):
    n = h.shape[0]
    tm = TOKEN_TILE
    row = lambda i: (i, 0)
    const = lambda i: (0, 0)

    def resident(a):
        return pl.BlockSpec(a.shape, const, pipeline_mode=pl.Buffered(1))

    return pl.pallas_call(
        functools.partial(_out_ffn_kernel, final_norm=final_norm),
        grid=(n // tm,),
        in_specs=[
            pl.BlockSpec((tm, D_MODEL), row),
            pl.BlockSpec((tm, ATTN_WIDTH), row),
            pl.BlockSpec((tm, SSM_INNER), row),
            resident(w_out), resident(norm_w), resident(w_gate), resident(w_up),
            resident(w_down), resident(final_w),
        ],
        out_specs=pl.BlockSpec((tm, D_MODEL), row),
        out_shape=jax.ShapeDtypeStruct((n, D_MODEL), F32),
        compiler_params=pltpu.CompilerParams(
            dimension_semantics=("parallel",), vmem_limit_bytes=VMEM_LIMIT_BYTES),
        name="out_ffn",
    )(h, attn, y, w_out, norm_w, w_gate, w_up, w_down, final_w)


def _rotary_tables(positions):
    inv_freq = ROPE_THETA ** (-jnp.arange(0, ROPE_DIM, 2, dtype=F32) / ROPE_DIM)
    ang = positions.astype(F32).reshape(-1, 1) * inv_freq
    cos, sin = jnp.cos(ang), jnp.sin(ang)
    n = ang.shape[0]
    pad = HEAD_DIM - ROPE_DIM
    ones = jnp.ones((n, pad), F32)
    zeros = jnp.zeros((n, pad), F32)
    z8 = jnp.zeros((n, ROPE_HALF), F32)
    rc = jnp.concatenate([cos, cos, ones], axis=1)
    rsa = jnp.concatenate([-sin, z8, zeros], axis=1)
    rsb = jnp.concatenate([z8, sin, zeros], axis=1)
    tile = lambda t: jnp.tile(t, (1, LANES // HEAD_DIM))
    return tile(rc), tile(rsa), tile(rsb)


def _lane_slab(v):
    return jnp.tile(v.astype(F32), DT_REPLICAS).reshape(1, LANES)


def _in_proj_weight(w):
    q_end = ATTN_WIDTH
    k_end = q_end + N_KV_HEADS * HEAD_DIM
    v_end = k_end + N_KV_HEADS * HEAD_DIM
    z_end = v_end + SSM_INNER
    xbc_end = z_end + CONV_CH

    def doubled(block):
        heads = [block[:, h * HEAD_DIM:(h + 1) * HEAD_DIM] for h in range(N_KV_HEADS)]
        return [t for h in heads for t in (h, h)]

    cols = ([w[:, :q_end]] + doubled(w[:, q_end:k_end]) + doubled(w[:, k_end:v_end])
            + [w[:, v_end:xbc_end], jnp.tile(w[:, xbc_end:], (1, DT_REPLICAS))])
    return jnp.concatenate(cols, axis=1).astype(BF16)


def kernel(x, positions, norm_mix, w_in, conv_w, conv_b, dt_bias, a_log, d_skip, ssm_norm,
           w_out, norm_ffn, w_gate, w_up, w_down, final_norm):
    bsz, seq, _ = x.shape
    depth = w_in.shape[0]
    n = bsz * seq
    assert seq % SSD_ROWS == 0 and n % TOKEN_TILE == 0
    assert DILATED_PATTERNS == ((128, 1), (512, 4), (2048, 16)) and seq % (16 * ATTN_BLOCK) == 0
    rc, rsa, rsb = _rotary_tables(positions)
    h = x.reshape(n, D_MODEL)
    for layer in range(depth):
        q, k, v, z, xbc, dt = _in_proj(h, norm_mix[layer].reshape(1, D_MODEL),
                                       _in_proj_weight(w_in[layer]), rc, rsa, rsb)
        attn = _attention(q, k, v, bsz, seq)
        a_neg = -jnp.exp(a_log[layer].astype(F32)) * LOG2E
        y = _ssd(xbc, z, dt, conv_w[layer], conv_b[layer].reshape(1, CONV_CH),
                 _lane_slab(dt_bias[layer]), _lane_slab(a_neg),
                 jnp.repeat(d_skip[layer].astype(F32), SSM_HEAD_DIM).reshape(1, SSM_INNER),
                 ssm_norm[layer].reshape(1, SSM_INNER), bsz, seq)
        h = _out_ffn(h, attn, y, w_out[layer].astype(BF16), norm_ffn[layer].reshape(1, D_MODEL),
                     w_gate[layer].astype(BF16), w_up[layer].astype(BF16),
                     w_down[layer].astype(BF16), final_norm.reshape(1, D_MODEL),
                     final_norm=(layer == depth - 1))
    return h.reshape(bsz, seq, D_MODEL)
```

```python
import functools
import math

import jax
import jax.numpy as jnp
import numpy as np
from jax import lax
from jax.experimental import pallas as pl
from jax.experimental.pallas import tpu as pltpu

F32 = jnp.float32
BF16 = jnp.bfloat16

D_MODEL = 1024
HEAD_DIM = 64
N_ATTN_HEADS = 8
N_KV_HEADS = 2
GQA_GROUP = N_ATTN_HEADS // N_KV_HEADS
ATTN_WIDTH = N_ATTN_HEADS * HEAD_DIM
ROPE_DIM = HEAD_DIM // 4
ROPE_HALF = ROPE_DIM // 2
ROPE_THETA = 500000.0
DILATED_PATTERNS = ((128, 1), (512, 4), (2048, 16))
ATTN_BLOCK = 128
SSM_HEAD_DIM = 64
SSM_HEADS = 16
SSM_INNER = SSM_HEADS * SSM_HEAD_DIM
SSM_GROUPS = 2
HEADS_PER_GROUP = SSM_HEADS // SSM_GROUPS
GROUP_WIDTH = SSM_INNER // SSM_GROUPS
D_STATE = 128
CONV_WIDTH = 4
CHUNK = 128
BC_WIDTH = SSM_GROUPS * D_STATE
CONV_CH = SSM_INNER + 2 * BC_WIDTH
MIX_WIDTH = ATTN_WIDTH + SSM_INNER
FFN_HIDDEN = 2816
EPS = 1e-5
LOG2E = math.log2(math.e)

LANES = 128
SUBLANES = 8
VMEM_LIMIT_BYTES = 56 * 1024 * 1024

Q_SLABS = ATTN_WIDTH // LANES
HEADS_PER_SLAB = LANES // HEAD_DIM
SLABS_PER_KV = GQA_GROUP // HEADS_PER_SLAB
XBC_SLABS = CONV_CH // LANES
X_SLABS = SSM_INNER // LANES
DT_REPLICAS = LANES // SSM_HEADS
NEG_INF = float("-inf")

COL_Q = 0
COL_K = COL_Q + ATTN_WIDTH
COL_V = COL_K + N_KV_HEADS * LANES
COL_Z = COL_V + N_KV_HEADS * LANES
COL_XBC = COL_Z + SSM_INNER
COL_DT = COL_XBC + CONV_CH
IN_PROJ_COLS = COL_DT + LANES

TOKEN_TILE = 512
SSD_ROWS = 512
FFN_CHUNK = 256
SORT_STRIDE = 4
BLOCKS_PER_TRIP = 4
SSD_CHUNKS_PER_TRIP = 4


def _dot(a, b):
    return jnp.dot(a, b, preferred_element_type=F32)


def _dot_nt(a, b):
    return lax.dot_general(a, b, (((1,), (1,)), ((), ())), preferred_element_type=F32)


def _split3(x):
    hi = x.astype(BF16)
    r1 = x - hi.astype(F32)
    mid = r1.astype(BF16)
    lo = (r1 - mid.astype(F32)).astype(BF16)
    return hi, mid, lo


def _silu(x):
    return x * (1.0 / (1.0 + jnp.exp(-x)))


def _softplus(x):
    return jnp.maximum(x, 0.0) + jnp.log(1.0 + jnp.exp(-jnp.abs(x)))


def _in_proj_kernel(x_ref, nw_ref, w_ref, rc_ref, rsa_ref, rsb_ref, cw_ref, cb_ref,
                    q_ref, k_ref, v_ref, z_ref, u_ref, dt_ref, pad_ref, *, tiles_per_seq):
    halo = SUBLANES

    @pl.when(pl.program_id(0) % tiles_per_seq == 0)
    def _():
        pad_ref[:, 0:halo, :] = jnp.zeros((XBC_SLABS, halo, LANES), F32)

    x = x_ref[...]
    ms = jnp.mean(x * x, axis=-1, keepdims=True)
    hn = (x * lax.rsqrt(ms + EPS) * nw_ref[...]).astype(BF16)
    rc = rc_ref[...]
    rsa = rsa_ref[...]
    rsb = rsb_ref[...]

    def rope(t):
        return (t * rc + pltpu.roll(t, LANES - ROPE_HALF, axis=1) * rsa
                + pltpu.roll(t, ROPE_HALF, axis=1) * rsb)

    xbc = _dot(hn, w_ref[:, COL_XBC:COL_DT])
    for s in range(XBC_SLABS):
        pad_ref[s, halo:halo + TOKEN_TILE, :] = xbc[:, s * LANES:(s + 1) * LANES]
    for c in range(TOKEN_TILE // CHUNK):
        base = c * CHUNK + halo - (CONV_WIDTH - 1)
        for s in range(XBC_SLABS):
            cols = slice(s * LANES, (s + 1) * LANES)
            taps = [pad_ref[s, pl.ds(base + t, CHUNK // 2, stride=2), :]
                    for t in range(CONV_WIDTH + 1)]
            for parity in range(2):
                u = cb_ref[:, cols] + sum(
                    taps[parity + t] * cw_ref[t:t + 1, cols] for t in range(CONV_WIDTH))
                u_ref[s, pl.ds(c * CHUNK + parity, CHUNK // 2, stride=2), :] = _silu(u)
    pad_ref[:, 0:halo, :] = pad_ref[:, TOKEN_TILE:TOKEN_TILE + halo, :]

    qk = _dot(hn, w_ref[:, COL_Q:COL_V])
    scale = HEAD_DIM ** -0.5 * LOG2E
    for s in range(Q_SLABS):
        q_ref[s] = rope(qk[:, s * LANES:(s + 1) * LANES]) * scale
    for h in range(N_KV_HEADS):
        k_ref[h] = rope(qk[:, COL_K + h * LANES:COL_K + (h + 1) * LANES])
    v = _dot(hn, w_ref[:, COL_V:COL_Z])
    for h in range(N_KV_HEADS):
        v_ref[h] = v[:, h * LANES:(h + 1) * LANES]
    z_ref[...] = _dot(hn, w_ref[:, COL_Z:COL_XBC])
    dt_ref[...] = _dot(hn, w_ref[:, COL_DT:IN_PROJ_COLS])


def _in_proj(h, norm_w, w, rc, rsa, rsb, conv_w, conv_b, seq):
    n = h.shape[0]
    tm = TOKEN_TILE
    row = lambda i: (i, 0)
    slab = lambda i: (0, i, 0)
    const = lambda i: (0, 0)
    return pl.pallas_call(
        functools.partial(_in_proj_kernel, tiles_per_seq=seq // tm),
        grid=(n // tm,),
        in_specs=[
            pl.BlockSpec((tm, D_MODEL), row),
            pl.BlockSpec((1, D_MODEL), const),
            pl.BlockSpec((D_MODEL, IN_PROJ_COLS), const),
            pl.BlockSpec((tm, LANES), row),
            pl.BlockSpec((tm, LANES), row),
            pl.BlockSpec((tm, LANES), row),
            pl.BlockSpec((CONV_WIDTH, CONV_CH), const),
            pl.BlockSpec((1, CONV_CH), const),
        ],
        out_specs=[
            pl.BlockSpec((Q_SLABS, tm, LANES), slab),
            pl.BlockSpec((N_KV_HEADS, tm, LANES), slab),
            pl.BlockSpec((N_KV_HEADS, tm, LANES), slab),
            pl.BlockSpec((tm, SSM_INNER), row),
            pl.BlockSpec((XBC_SLABS, tm, LANES), slab),
            pl.BlockSpec((tm, LANES), row),
        ],
        out_shape=[
            jax.ShapeDtypeStruct((Q_SLABS, n, LANES), F32),
            jax.ShapeDtypeStruct((N_KV_HEADS, n, LANES), F32),
            jax.ShapeDtypeStruct((N_KV_HEADS, n, LANES), F32),
            jax.ShapeDtypeStruct((n, SSM_INNER), F32),
            jax.ShapeDtypeStruct((XBC_SLABS, n, LANES), F32),
            jax.ShapeDtypeStruct((n, LANES), F32),
        ],
        scratch_shapes=[pltpu.VMEM((XBC_SLABS, tm + 2 * SUBLANES, LANES), F32)],
        compiler_params=pltpu.CompilerParams(
            dimension_semantics=("arbitrary",), vmem_limit_bytes=VMEM_LIMIT_BYTES),
        name="in_proj",
    )(h, norm_w, w, rc, rsa, rsb, conv_w, conv_b)


def _attn_kernel(q_ref, k_ref, v_ref, bias_ref, o_ref,
                 qs_ref, ks_ref, vs_ref, nat_ref, srt_ref, p_ref, mx_ref, *, seq):
    lane = lax.broadcasted_iota(jnp.int32, (ATTN_BLOCK, LANES), 1)
    low = lane < HEAD_DIM
    sorted_rows = seq // SORT_STRIDE
    half = SLABS_PER_KV * ATTN_BLOCK
    ACC, MAX, DEN = 0, 1, 2

    def ds(start, stride):
        if stride == 1:
            return pl.ds(start, ATTN_BLOCK)
        return pl.ds(start, ATTN_BLOCK, stride=stride)

    def aligned(start):
        return ds(pl.multiple_of(start, ATTN_BLOCK), 1)

    def gather(src, q_rows, prev_rows):
        cur = src[q_rows, :]
        if prev_rows is None:
            return cur
        return jnp.concatenate([src[prev_rows, :], cur], axis=0)

    def score_stage(slot, i, q_src, k_src, q_rows, prev_rows, bias):
        keys = gather(k_src, q_rows, prev_rows)
        slabs = [q_src[j, q_rows, :] for j in range(SLABS_PER_KV)]
        q4 = jnp.concatenate([jnp.where(low, t, 0.0) for t in slabs]
                             + [jnp.where(low, 0.0, t) for t in slabs], axis=0)
        s = _dot_nt(q4.astype(BF16), keys.astype(BF16)) + bias
        m_new = jnp.max(s, axis=-1, keepdims=True)
        p_ref[slot, i, :, 0:keys.shape[0]] = jnp.exp2(s - m_new).astype(BF16)
        for j in range(SLABS_PER_KV):
            lo = m_new[j * ATTN_BLOCK:(j + 1) * ATTN_BLOCK]
            hi = m_new[half + j * ATTN_BLOCK:half + (j + 1) * ATTN_BLOCK]
            mx_ref[slot, i, j] = jnp.where(low, lo, hi)

    def value_stage(slot, i, v_src, q_rows, prev_rows, load_old, store_new):
        vals = gather(v_src, q_rows, prev_rows)
        lane_k = lax.broadcasted_iota(jnp.int32, vals.shape, 1) < HEAD_DIM
        vals_lo = jnp.where(lane_k, vals, 1.0).astype(BF16)
        vals_hi = jnp.where(lane_k, 1.0, vals).astype(BF16)
        nk = vals.shape[0]
        pv_lo = _dot(p_ref[slot, i, 0:half, 0:nk], vals_lo)
        pv_hi = _dot(p_ref[slot, i, half:2 * half, 0:nk], vals_hi)
        for j in range(SLABS_PER_KV):
            r = slice(j * ATTN_BLOCK, (j + 1) * ATTN_BLOCK)
            acc_n = jnp.where(low, pv_lo[r], pv_hi[r])
            den_n = pltpu.roll(jnp.where(low, pv_hi[r], pv_lo[r]), HEAD_DIM, axis=1)
            max_n = mx_ref[slot, i, j]
            if load_old is None:
                store_new(j, acc_n, max_n, den_n)
            else:
                acc_o, max_o, den_o = load_old(j)
                max_t = jnp.maximum(max_o, max_n)
                w_o = jnp.exp2(max_o - max_t)
                w_n = jnp.exp2(max_n - max_t)
                store_new(j, acc_o * w_o + acc_n * w_n, max_t, den_o * w_o + den_n * w_n)

    def windowed_bias(nb):
        if isinstance(nb, int):
            return bias_ref[min(nb, 1)]
        return bias_ref[jnp.minimum(nb, 1)]

    def prev_block(nb):
        return max(nb - 1, 0) if isinstance(nb, int) else jnp.maximum(nb - 1, 0)

    def rows1(g, i):
        nb = g * BLOCKS_PER_TRIP + i
        return nb, aligned(nb * ATTN_BLOCK), aligned(prev_block(nb) * ATTN_BLOCK)

    def score1(g, slot):
        for i in range(BLOCKS_PER_TRIP):
            nb, rows, prev_rows = rows1(g, i)
            score_stage(slot, i, q_ref, k_ref, rows, prev_rows, windowed_bias(nb))

    def value1(g, slot):
        for i in range(BLOCKS_PER_TRIP):
            _, rows, prev_rows = rows1(g, i)

            def store(j, acc, mx, den, rows=rows):
                nat_ref[ACC, j, rows, :] = acc
                nat_ref[MAX, j, rows, :] = mx
                nat_ref[DEN, j, rows, :] = den

            value_stage(slot, i, v_ref, rows, prev_rows, None, store)

    assert sorted_rows // ATTN_BLOCK == BLOCKS_PER_TRIP

    def rows2(r4, nb):
        return (aligned(r4 * sorted_rows + nb * ATTN_BLOCK),
                aligned(r4 * sorted_rows + prev_block(nb) * ATTN_BLOCK))

    def score2(r4, slot):
        for nb in range(BLOCKS_PER_TRIP):
            rows, prev_rows = rows2(r4, nb)
            score_stage(slot, nb, qs_ref, ks_ref, rows, prev_rows, windowed_bias(nb))

    def value2(r4, slot):
        for nb in range(BLOCKS_PER_TRIP):
            rows, prev_rows = rows2(r4, nb)
            nat_rows = ds(r4 + nb * (ATTN_BLOCK * SORT_STRIDE), SORT_STRIDE)

            def load(j, nat_rows=nat_rows):
                return tuple(nat_ref[t, j, nat_rows, :] for t in (ACC, MAX, DEN))

            def store(j, acc, mx, den, rows=rows):
                srt_ref[ACC, j, rows, :] = acc
                srt_ref[MAX, j, rows, :] = mx
                srt_ref[DEN, j, rows, :] = den

            value_stage(slot, nb, vs_ref, rows, prev_rows, load, store)

    def rows3(g, i):
        r16 = g * BLOCKS_PER_TRIP + i
        start = (r16 % SORT_STRIDE) * sorted_rows + r16 // SORT_STRIDE
        return r16, ds(start, SORT_STRIDE)

    def score3(g, slot):
        for i in range(BLOCKS_PER_TRIP):
            _, rows = rows3(g, i)
            score_stage(slot, i, qs_ref, ks_ref, rows, None, bias_ref[1, :, ATTN_BLOCK:])

    def value3(g, slot):
        for i in range(BLOCKS_PER_TRIP):
            r16, rows = rows3(g, i)
            out_rows = ds(r16, SORT_STRIDE * SORT_STRIDE)

            def load(j, rows=rows):
                return tuple(srt_ref[t, j, rows, :] for t in (ACC, MAX, DEN))

            def store(j, acc, mx, den, out_rows=out_rows):
                nat_ref[ACC, j, out_rows, :] = acc * (1.0 / den)

            value_stage(slot, i, vs_ref, rows, None, load, store)

    def sort_rows(i, c):
        r4 = i // (sorted_rows // ATTN_BLOCK)
        blk = i % (sorted_rows // ATTN_BLOCK)
        src = ds(r4 + blk * (ATTN_BLOCK * SORT_STRIDE), SORT_STRIDE)
        dst = aligned(i * ATTN_BLOCK)
        for j in range(SLABS_PER_KV):
            qs_ref[j, dst, :] = q_ref[j, src, :]
        ks_ref[dst, :] = k_ref[src, :]
        vs_ref[dst, :] = v_ref[src, :]
        return c

    lax.fori_loop(0, seq // ATTN_BLOCK, sort_rows, 0)

    branches = ((score1, value1, seq // ATTN_BLOCK // BLOCKS_PER_TRIP),
                (score2, value2, SORT_STRIDE),
                (score3, value3, SORT_STRIDE * SORT_STRIDE // BLOCKS_PER_TRIP))
    assert all(n % 2 == 0 for _, _, n in branches)
    branches[0][0](0, 0)
    for idx, (score, value, n_groups) in enumerate(branches):
        def body(g, c, score=score, value=value):
            value(g, g % 2)
            score(g + 1, (g + 1) % 2)
            return c

        lax.fori_loop(0, n_groups - 1, body, 0)
        value(n_groups - 1, 1)
        if idx + 1 < len(branches):
            branches[idx + 1][0](0, 0)

    out_chunk = 256

    def finish(i, c):
        r = pl.ds(pl.multiple_of(i * out_chunk, out_chunk), out_chunk)
        for j in range(SLABS_PER_KV):
            o_ref[r, j * LANES:(j + 1) * LANES] = nat_ref[ACC, j, r, :].astype(o_ref.dtype)
        return c

    lax.fori_loop(0, seq // out_chunk, finish, 0)


def _attn_bias():
    qi = np.arange(GQA_GROUP * ATTN_BLOCK)[:, None] % ATTN_BLOCK
    ki = np.arange(2 * ATTN_BLOCK)[None, :]
    own = (ki >= ATTN_BLOCK) & (ki - ATTN_BLOCK <= qi)
    valid = np.stack([own, own | ((ki < ATTN_BLOCK) & (ki >= qi))])
    return jnp.asarray(np.where(valid, 0.0, NEG_INF), F32)


def _attention(q, k, v, bsz, seq):
    n = bsz * seq
    sq = pl.Squeezed()
    return pl.pallas_call(
        functools.partial(_attn_kernel, seq=seq),
        grid=(bsz, N_KV_HEADS),
        in_specs=[
            pl.BlockSpec((SLABS_PER_KV, seq, LANES), lambda b, h: (h, b, 0)),
            pl.BlockSpec((sq, seq, LANES), lambda b, h: (h, b, 0)),
            pl.BlockSpec((sq, seq, LANES), lambda b, h: (h, b, 0)),
            pl.BlockSpec((2, GQA_GROUP * ATTN_BLOCK, 2 * ATTN_BLOCK), lambda b, h: (0, 0, 0)),
        ],
        out_specs=pl.BlockSpec((seq, SLABS_PER_KV * LANES), lambda b, h: (b, h)),
        out_shape=jax.ShapeDtypeStruct((n, ATTN_WIDTH), BF16),
        scratch_shapes=[
            pltpu.VMEM((SLABS_PER_KV, seq, LANES), F32),
            pltpu.VMEM((seq, LANES), F32),
            pltpu.VMEM((seq, LANES), F32),
            pltpu.VMEM((3, SLABS_PER_KV, seq, LANES), F32),
            pltpu.VMEM((3, SLABS_PER_KV, seq, LANES), F32),
            pltpu.VMEM((2, BLOCKS_PER_TRIP, GQA_GROUP * ATTN_BLOCK, 2 * ATTN_BLOCK), BF16),
            pltpu.VMEM((2, BLOCKS_PER_TRIP, SLABS_PER_KV, ATTN_BLOCK, LANES), F32),
        ],
        compiler_params=pltpu.CompilerParams(
            dimension_semantics=("parallel", "parallel"), vmem_limit_bytes=VMEM_LIMIT_BYTES),
        name="attn",
    )(q, k, v, _attn_bias())


def _ssd_kernel(u_ref, dt_ref, dtb_ref, aneg_ref, dskip_ref,
                tri_ref, sel_ref, expand_ref, lbias_ref,
                o_ref, state_ref, spread_ref, cst_ref, *, seq):
    j = pl.program_id(1)
    n_chunks = SSD_ROWS // CHUNK

    @pl.when(j == 0)
    def _():
        state_ref[...] = jnp.zeros(state_ref.shape, F32)

    lane = lax.broadcasted_iota(jnp.int32, (CHUNK, LANES), 1)
    low_half = lane < SSM_HEAD_DIM
    lane_group = lane // SSM_HEADS
    tri = tri_ref[...]
    lbias = lbias_ref[...]

    chunks_per_seq = seq // CHUNK

    def decay_stage(cg, slot):
        rows = pl.ds(pl.multiple_of(cg * CHUNK, CHUNK), CHUNK)
        dt = _softplus(dt_ref[rows, :] + dtb_ref[...])
        a = dt * aneg_ref[...]
        cs = sum(_dot(tri, t) for t in _split3(a))
        terms = _split3(cs) + _split3(dt)
        packed = jnp.zeros((CHUNK, LANES), BF16)
        for g, t in enumerate(terms):
            packed = jnp.where(lane_group == g, t, packed)
        spread_ref[slot] = _dot(packed, expand_ref[...])
        cs_t3 = _dot_nt(sel_ref[...], packed)
        cst_ref[slot] = (cs_t3[0:SSM_HEADS] + cs_t3[SSM_HEADS:2 * SSM_HEADS]
                         + cs_t3[2 * SSM_HEADS:3 * SSM_HEADS])

    def scan_stage(c, slot):
        rows = pl.ds(pl.multiple_of(c * CHUNK, CHUNK), CHUNK)
        x = jnp.concatenate([u_ref[s, rows, :] for s in range(X_SLABS)], axis=1)
        b_slab = X_SLABS
        c_slab = X_SLABS + SSM_GROUPS

        def spread(e):
            return spread_ref[slot, :, e * LANES:(e + 1) * LANES]

        cs_t = cst_ref[slot]
        bc_cols = SSM_HEADS * LANES
        cs_x = jnp.concatenate(
            [jnp.where(low_half, spread(2 * i), spread(2 * i + 1))
             for i in range(SSM_HEADS // 2)], axis=1)
        dt_x = spread_ref[slot, :, bc_cols:bc_cols + SSM_INNER]
        last = cs_x[CHUNK - 1:CHUNK, :]
        xdt = x * dt_x
        xdt_b = xdt.astype(BF16)
        xw_b = (xdt * jnp.exp2(last - cs_x)).astype(BF16)
        exp_cs = jnp.exp2(cs_x)
        chunk_decay = jnp.exp2(last)

        y_parts = []
        for g in range(SSM_GROUPS):
            b_g = u_ref[b_slab + g, rows, :]
            c_g = u_ref[c_slab + g, rows, :].astype(BF16)
            cb = _dot_nt(c_g, b_g.astype(BF16))
            gcols = slice(g * GROUP_WIDTH, (g + 1) * GROUP_WIDTH)
            state = state_ref[g]
            y_g = _dot(c_g, state.astype(BF16)) * exp_cs[:, gcols]
            state_ref[g] = state * chunk_decay[:, gcols] + _dot(b_g.T.astype(BF16), xw_b[:, gcols])
            diag = []
            for i in range(HEADS_PER_GROUP // 2):
                pair = g * (HEADS_PER_GROUP // 2) + i
                ms = []
                for e in (2 * pair, 2 * pair + 1):
                    diff = spread(e) - cs_t[e:e + 1, :]
                    ms.append((cb * jnp.exp2(diff + lbias)).astype(BF16))
                xs = xdt_b[:, pair * LANES:(pair + 1) * LANES]
                zero = jnp.zeros_like(xs)
                rhs = jnp.concatenate(
                    [jnp.where(low_half, xs, zero), jnp.where(low_half, zero, xs)], axis=0)
                diag.append(_dot(jnp.concatenate(ms, axis=1), rhs))
            y_parts.append(y_g + jnp.concatenate(diag, axis=1))
        o_ref[rows, :] = jnp.concatenate(y_parts, axis=1) + dskip_ref[...] * x

    @pl.when(j == 0)
    def _():
        decay_stage(0, 0)

    assert SSD_CHUNKS_PER_TRIP % 2 == 0 and n_chunks % SSD_CHUNKS_PER_TRIP == 0

    def chunk_group(g, carry):
        for i in range(SSD_CHUNKS_PER_TRIP):
            c = g * SSD_CHUNKS_PER_TRIP + i
            cg = j * n_chunks + c
            scan_stage(c, i % 2)
            decay_stage(jnp.minimum(cg + 1, chunks_per_seq - 1), (i + 1) % 2)
        return carry

    lax.fori_loop(0, n_chunks // SSD_CHUNKS_PER_TRIP, chunk_group, 0)


def _ssd_constants():
    tri = np.tril(np.ones((CHUNK, CHUNK), np.float32))
    expand = np.zeros((LANES, SSM_HEADS * LANES + SSM_INNER), np.float32)
    for lane in range(6 * SSM_HEADS):
        g, e = divmod(lane, SSM_HEADS)
        if g < 3:
            expand[lane, e * LANES:(e + 1) * LANES] = 1.0
        else:
            c0 = SSM_HEADS * LANES + e * SSM_HEAD_DIM
            expand[lane, c0:c0 + SSM_HEAD_DIM] = 1.0
    sel = np.eye(3 * SSM_HEADS, LANES, dtype=np.float32)
    lbias = np.where(np.tril(np.ones((CHUNK, CHUNK), bool)), 0.0, NEG_INF).astype(np.float32)
    return (jnp.asarray(tri, BF16), jnp.asarray(sel, BF16), jnp.asarray(expand, BF16),
            jnp.asarray(lbias, F32))


def _ssd(u, dt, dt_bias, a_neg, d_skip, bsz, seq):
    n = bsz * seq
    steps = seq // SSD_ROWS
    row = lambda b, j: (b * steps + j, 0)
    const = lambda b, j: (0, 0)
    tri, sel, expand, lbias = _ssd_constants()

    def full(a):
        return pl.BlockSpec(a.shape, const)

    consts = (dt_bias, a_neg, d_skip, tri, sel, expand, lbias)
    return pl.pallas_call(
        functools.partial(_ssd_kernel, seq=seq),
        grid=(bsz, steps),
        in_specs=[
            pl.BlockSpec((XBC_SLABS, SSD_ROWS, LANES), lambda b, j: (0, b * steps + j, 0)),
            pl.BlockSpec((seq, LANES), lambda b, j: (b, 0)),
        ] + [full(a) for a in consts],
        out_specs=pl.BlockSpec((SSD_ROWS, SSM_INNER), row),
        out_shape=jax.ShapeDtypeStruct((n, SSM_INNER), F32),
        scratch_shapes=[
            pltpu.VMEM((SSM_GROUPS, D_STATE, GROUP_WIDTH), F32),
            pltpu.VMEM((2, CHUNK, SSM_HEADS * LANES + SSM_INNER), F32),
            pltpu.VMEM((2, SSM_HEADS, LANES), F32),
        ],
        compiler_params=pltpu.CompilerParams(
            dimension_semantics=("parallel", "arbitrary"), vmem_limit_bytes=VMEM_LIMIT_BYTES),
        name="ssd",
    )(u, dt, *consts)


def _out_ffn_kernel(h_ref, attn_ref, y_ref, z_ref, sw_ref, wo_ref, nw_ref, wg_ref, wu_ref, wd_ref,
                    fw_ref, o_ref, *, final_norm):
    gated = y_ref[...] * _silu(z_ref[...])
    parts = []
    for g in range(SSM_GROUPS):
        y_g = gated[:, g * GROUP_WIDTH:(g + 1) * GROUP_WIDTH]
        ms = jnp.mean(y_g * y_g, axis=-1, keepdims=True)
        parts.append(y_g * lax.rsqrt(ms + EPS))
    y = (jnp.concatenate(parts, axis=1) * sw_ref[...]).astype(BF16)
    h1 = (h_ref[...] + _dot(attn_ref[...], wo_ref[0:ATTN_WIDTH, :])
          + _dot(y, wo_ref[ATTN_WIDTH:MIX_WIDTH, :]))
    ms = jnp.mean(h1 * h1, axis=-1, keepdims=True)
    hn = (h1 * lax.rsqrt(ms + EPS) * nw_ref[...]).astype(BF16)
    ffn = jnp.zeros_like(h1)
    for c in range(FFN_HIDDEN // FFN_CHUNK):
        cols = slice(c * FFN_CHUNK, (c + 1) * FFN_CHUNK)
        gate = _dot(hn, wg_ref[:, cols])
        up = _dot(hn, wu_ref[:, cols])
        ffn = ffn + _dot((_silu(gate) * up).astype(BF16), wd_ref[cols, :])
    acc = h1 + ffn
    if final_norm:
        ms = jnp.mean(acc * acc, axis=-1, keepdims=True)
        acc = acc * lax.rsqrt(ms + EPS) * fw_ref[...]
    o_ref[...] = acc


def _out_ffn(h, attn, y, z, ssm_w, w_out, norm_w, w_gate, w_up, w_down, final_w, final_norm):
    n = h.shape[0]
    tm = TOKEN_TILE
    row = lambda i: (i, 0)
    const = lambda i: (0, 0)

    def resident(a):
        return pl.BlockSpec(a.shape, const, pipeline_mode=pl.Buffered(1))

    return pl.pallas_call(
        functools.partial(_out_ffn_kernel, final_norm=final_norm),
        grid=(n // tm,),
        in_specs=[
            pl.BlockSpec((tm, D_MODEL), row),
            pl.BlockSpec((tm, ATTN_WIDTH), row),
            pl.BlockSpec((tm, SSM_INNER), row),
            pl.BlockSpec((tm, SSM_INNER), row),
            resident(ssm_w), resident(w_out), resident(norm_w), resident(w_gate), resident(w_up),
            resident(w_down), resident(final_w),
        ],
        out_specs=pl.BlockSpec((tm, D_MODEL), row),
        out_shape=jax.ShapeDtypeStruct((n, D_MODEL), F32),
        compiler_params=pltpu.CompilerParams(
            dimension_semantics=("parallel",), vmem_limit_bytes=VMEM_LIMIT_BYTES),
        name="out_ffn",
    )(h, attn, y, z, ssm_w, w_out, norm_w, w_gate, w_up, w_down, final_w)


def _rotary_tables(positions):
    lane = np.arange(LANES) % HEAD_DIM
    rotated = lane < ROPE_DIM
    inv_freq = ROPE_THETA ** (-jnp.arange(0, ROPE_DIM, 2, dtype=F32) / ROPE_DIM)
    freq = jnp.where(rotated, inv_freq[lane % ROPE_HALF], 0.0)
    ang = positions.astype(F32).reshape(-1, 1) * freq
    cos, sin = jnp.cos(ang), jnp.sin(ang)
    first = jnp.asarray(lane < ROPE_HALF)
    return cos, jnp.where(first, -sin, 0.0), jnp.where(first, 0.0, sin)


def _lane_slab(v):
    return jnp.tile(v.astype(F32), DT_REPLICAS).reshape(1, LANES)


def _in_proj_weight(w):
    q_end = ATTN_WIDTH
    k_end = q_end + N_KV_HEADS * HEAD_DIM
    v_end = k_end + N_KV_HEADS * HEAD_DIM
    z_end = v_end + SSM_INNER
    xbc_end = z_end + CONV_CH

    def doubled(block):
        heads = [block[:, h * HEAD_DIM:(h + 1) * HEAD_DIM] for h in range(N_KV_HEADS)]
        return [t for h in heads for t in (h, h)]

    cols = ([w[:, :q_end]] + doubled(w[:, q_end:k_end]) + doubled(w[:, k_end:v_end])
            + [w[:, v_end:xbc_end], jnp.tile(w[:, xbc_end:], (1, DT_REPLICAS))])
    return jnp.concatenate(cols, axis=1).astype(BF16)


def kernel(x, positions, norm_mix, w_in, conv_w, conv_b, dt_bias, a_log, d_skip, ssm_norm,
           w_out, norm_ffn, w_gate, w_up, w_down, final_norm):
    bsz, seq, _ = x.shape
    depth = w_in.shape[0]
    n = bsz * seq
    assert seq % SSD_ROWS == 0 and seq % TOKEN_TILE == 0
    assert DILATED_PATTERNS == ((128, 1), (512, 4), (2048, 16)) and seq % (16 * ATTN_BLOCK) == 0
    rc, rsa, rsb = _rotary_tables(positions)
    h = x.reshape(n, D_MODEL)
    for layer in range(depth):
        q, k, v, z, u, dt = _in_proj(h, norm_mix[layer].reshape(1, D_MODEL),
                                     _in_proj_weight(w_in[layer]), rc, rsa, rsb,
                                     conv_w[layer], conv_b[layer].reshape(1, CONV_CH), seq)
        attn = _attention(q, k, v, bsz, seq)
        a_neg = -jnp.exp(a_log[layer].astype(F32)) * LOG2E
        y = _ssd(u, dt, _lane_slab(dt_bias[layer]), _lane_slab(a_neg),
                 jnp.repeat(d_skip[layer].astype(F32), SSM_HEAD_DIM).reshape(1, SSM_INNER), bsz, seq)
        h = _out_ffn(h, attn, y, z, ssm_norm[layer].reshape(1, SSM_INNER), w_out[layer].astype(BF16), norm_ffn[layer].reshape(1, D_MODEL),
                     w_gate[layer].astype(BF16), w_up[layer].astype(BF16),
                     w_down[layer].astype(BF16), final_norm.reshape(1, D_MODEL),
                     final_norm=(layer == depth - 1))
    return h.reshape(bsz, seq, D_MODEL)
```

```python
import functools
import math

import jax
import jax.numpy as jnp
import numpy as np
from jax import lax
from jax.experimental import pallas as pl
from jax.experimental.pallas import tpu as pltpu

F32 = jnp.float32
BF16 = jnp.bfloat16

D_MODEL = 1024
HEAD_DIM = 64
N_ATTN_HEADS = 8
N_KV_HEADS = 2
GQA_GROUP = N_ATTN_HEADS // N_KV_HEADS
ATTN_WIDTH = N_ATTN_HEADS * HEAD_DIM
ROPE_DIM = HEAD_DIM // 4
ROPE_HALF = ROPE_DIM // 2
ROPE_THETA = 500000.0
DILATED_PATTERNS = ((128, 1), (512, 4), (2048, 16))
ATTN_BLOCK = 128
SSM_HEAD_DIM = 64
SSM_HEADS = 16
SSM_INNER = SSM_HEADS * SSM_HEAD_DIM
SSM_GROUPS = 2
HEADS_PER_GROUP = SSM_HEADS // SSM_GROUPS
GROUP_WIDTH = SSM_INNER // SSM_GROUPS
D_STATE = 128
CONV_WIDTH = 4
CHUNK = 128
BC_WIDTH = SSM_GROUPS * D_STATE
CONV_CH = SSM_INNER + 2 * BC_WIDTH
MIX_WIDTH = ATTN_WIDTH + SSM_INNER
FFN_HIDDEN = 2816
EPS = 1e-5
LOG2E = math.log2(math.e)

LANES = 128
SUBLANES = 8
VMEM_LIMIT_BYTES = 56 * 1024 * 1024

Q_SLABS = ATTN_WIDTH // LANES
HEADS_PER_SLAB = LANES // HEAD_DIM
SLABS_PER_KV = GQA_GROUP // HEADS_PER_SLAB
XBC_SLABS = CONV_CH // LANES
X_SLABS = SSM_INNER // LANES
DT_REPLICAS = LANES // SSM_HEADS
NEG_INF = float("-inf")

COL_Q = 0
COL_K = COL_Q + ATTN_WIDTH
COL_V = COL_K + N_KV_HEADS * HEAD_DIM
COL_Z = COL_V + N_KV_HEADS * HEAD_DIM
COL_XBC = COL_Z + SSM_INNER
COL_DT = COL_XBC + CONV_CH
IN_PROJ_COLS = COL_DT + LANES

TOKEN_TILE = 512
SSD_ROWS = 1024
FFN_CHUNK = 256
SORT_STRIDE = 4
BLOCKS_PER_TRIP = 4
SSD_CHUNKS_PER_TRIP = 4


def _dot(a, b):
    return jnp.dot(a, b, preferred_element_type=F32)


def _dot_nt(a, b):
    return lax.dot_general(a, b, (((1,), (1,)), ((), ())), preferred_element_type=F32)


def _split3(x):
    hi = x.astype(BF16)
    r1 = x - hi.astype(F32)
    mid = r1.astype(BF16)
    lo = (r1 - mid.astype(F32)).astype(BF16)
    return hi, mid, lo


def _silu(x):
    return x * (1.0 / (1.0 + jnp.exp(-x)))


def _softplus(x):
    return jnp.maximum(x, 0.0) + jnp.log(1.0 + jnp.exp(-jnp.abs(x)))


def _in_proj_kernel(x_ref, nw_ref, w_ref, rc_ref, rsa_ref, rsb_ref, cw_ref, cb_ref,
                    q_ref, k_ref, v_ref, z_ref, u_ref, dt_ref, pad_ref, *, tiles_per_seq):
    halo = SUBLANES

    @pl.when(pl.program_id(0) % tiles_per_seq == 0)
    def _():
        pad_ref[:, 0:halo, :] = jnp.zeros((XBC_SLABS, halo, LANES), F32)

    x = x_ref[...]
    ms = jnp.mean(x * x, axis=-1, keepdims=True)
    hn = (x * lax.rsqrt(ms + EPS) * nw_ref[...]).astype(BF16)
    rc = rc_ref[...]
    rsa = rsa_ref[...]
    rsb = rsb_ref[...]

    def rope(t):
        return (t * rc + pltpu.roll(t, LANES - ROPE_HALF, axis=1) * rsa
                + pltpu.roll(t, ROPE_HALF, axis=1) * rsb)

    xbc = _dot(hn, w_ref[:, COL_XBC:COL_DT])
    for s in range(XBC_SLABS):
        pad_ref[s, halo:halo + TOKEN_TILE, :] = xbc[:, s * LANES:(s + 1) * LANES]
    for c in range(TOKEN_TILE // CHUNK):
        base = c * CHUNK + halo - (CONV_WIDTH - 1)
        for s in range(XBC_SLABS):
            cols = slice(s * LANES, (s + 1) * LANES)
            taps = [pad_ref[s, pl.ds(base + t, CHUNK // 2, stride=2), :]
                    for t in range(CONV_WIDTH + 1)]
            for parity in range(2):
                u = cb_ref[:, cols] + sum(
                    taps[parity + t] * cw_ref[t:t + 1, cols] for t in range(CONV_WIDTH))
                u_ref[s, pl.ds(c * CHUNK + parity, CHUNK // 2, stride=2), :] = _silu(u)
    pad_ref[:, 0:halo, :] = pad_ref[:, TOKEN_TILE:TOKEN_TILE + halo, :]

    q = _dot(hn, w_ref[:, COL_Q:COL_K])
    scale = HEAD_DIM ** -0.5 * LOG2E
    for s in range(Q_SLABS):
        q_ref[s] = rope(q[:, s * LANES:(s + 1) * LANES]) * scale
    kv = _dot(hn, w_ref[:, COL_K:COL_Z])
    low = lax.broadcasted_iota(jnp.int32, (TOKEN_TILE, LANES), 1) < HEAD_DIM
    for ref, t in ((k_ref, rope(kv[:, 0:LANES])), (v_ref, kv[:, LANES:2 * LANES])):
        swapped = pltpu.roll(t, HEAD_DIM, axis=1)
        ref[0] = jnp.where(low, t, swapped)
        ref[1] = jnp.where(low, swapped, t)
    z_ref[...] = _dot(hn, w_ref[:, COL_Z:COL_XBC])
    dt_ref[...] = _dot(hn, w_ref[:, COL_DT:IN_PROJ_COLS])


def _in_proj(h, norm_w, w, rc, rsa, rsb, conv_w, conv_b, seq):
    n = h.shape[0]
    tm = TOKEN_TILE
    row = lambda i: (i, 0)
    slab = lambda i: (0, i, 0)
    const = lambda i: (0, 0)
    return pl.pallas_call(
        functools.partial(_in_proj_kernel, tiles_per_seq=seq // tm),
        grid=(n // tm,),
        in_specs=[
            pl.BlockSpec((tm, D_MODEL), row),
            pl.BlockSpec((1, D_MODEL), const),
            pl.BlockSpec((D_MODEL, IN_PROJ_COLS), const),
            pl.BlockSpec((tm, LANES), row),
            pl.BlockSpec((tm, LANES), row),
            pl.BlockSpec((tm, LANES), row),
            pl.BlockSpec((CONV_WIDTH, CONV_CH), const),
            pl.BlockSpec((1, CONV_CH), const),
        ],
        out_specs=[
            pl.BlockSpec((Q_SLABS, tm, LANES), slab),
            pl.BlockSpec((N_KV_HEADS, tm, LANES), slab),
            pl.BlockSpec((N_KV_HEADS, tm, LANES), slab),
            pl.BlockSpec((tm, SSM_INNER), row),
            pl.BlockSpec((XBC_SLABS, tm, LANES), slab),
            pl.BlockSpec((tm, LANES), row),
        ],
        out_shape=[
            jax.ShapeDtypeStruct((Q_SLABS, n, LANES), F32),
            jax.ShapeDtypeStruct((N_KV_HEADS, n, LANES), F32),
            jax.ShapeDtypeStruct((N_KV_HEADS, n, LANES), F32),
            jax.ShapeDtypeStruct((n, SSM_INNER), F32),
            jax.ShapeDtypeStruct((XBC_SLABS, n, LANES), F32),
            jax.ShapeDtypeStruct((n, LANES), F32),
        ],
        scratch_shapes=[pltpu.VMEM((XBC_SLABS, tm + 2 * SUBLANES, LANES), F32)],
        compiler_params=pltpu.CompilerParams(
            dimension_semantics=("arbitrary",), vmem_limit_bytes=VMEM_LIMIT_BYTES),
        name="in_proj",
    )(h, norm_w, w, rc, rsa, rsb, conv_w, conv_b)


def _attn_kernel(q_ref, k_ref, v_ref, bias_ref, o_ref,
                 qs_ref, ks_ref, vs_ref, nat_ref, srt_ref, p_ref, mx_ref, *, seq):
    lane = lax.broadcasted_iota(jnp.int32, (ATTN_BLOCK, LANES), 1)
    low = lane < HEAD_DIM
    sorted_rows = seq // SORT_STRIDE
    half = SLABS_PER_KV * ATTN_BLOCK
    ACC, MAX, DEN = 0, 1, 2

    def ds(start, stride):
        if stride == 1:
            return pl.ds(start, ATTN_BLOCK)
        return pl.ds(start, ATTN_BLOCK, stride=stride)

    def aligned(start):
        return ds(pl.multiple_of(start, ATTN_BLOCK), 1)

    def gather(src, q_rows, prev_rows):
        cur = src[q_rows, :]
        if prev_rows is None:
            return cur
        return jnp.concatenate([src[prev_rows, :], cur], axis=0)

    def score_stage(slot, i, q_src, k_src, q_rows, prev_rows, bias):
        keys = gather(k_src, q_rows, prev_rows)
        slabs = [q_src[j, q_rows, :] for j in range(SLABS_PER_KV)]
        q4 = jnp.concatenate([jnp.where(low, t, 0.0) for t in slabs]
                             + [jnp.where(low, 0.0, t) for t in slabs], axis=0)
        s = _dot_nt(q4.astype(BF16), keys.astype(BF16)) + bias
        m_new = jnp.max(s, axis=-1, keepdims=True)
        p_ref[slot, i, :, 0:keys.shape[0]] = jnp.exp2(s - m_new).astype(BF16)
        for j in range(SLABS_PER_KV):
            lo = m_new[j * ATTN_BLOCK:(j + 1) * ATTN_BLOCK]
            hi = m_new[half + j * ATTN_BLOCK:half + (j + 1) * ATTN_BLOCK]
            mx_ref[slot, i, j] = jnp.where(low, lo, hi)

    def value_stage(slot, i, v_src, q_rows, prev_rows, load_old, store_new):
        vals = gather(v_src, q_rows, prev_rows)
        lane_k = lax.broadcasted_iota(jnp.int32, vals.shape, 1) < HEAD_DIM
        vals_lo = jnp.where(lane_k, vals, 1.0).astype(BF16)
        vals_hi = jnp.where(lane_k, 1.0, vals).astype(BF16)
        nk = vals.shape[0]
        pv_lo = _dot(p_ref[slot, i, 0:half, 0:nk], vals_lo)
        pv_hi = _dot(p_ref[slot, i, half:2 * half, 0:nk], vals_hi)
        for j in range(SLABS_PER_KV):
            r = slice(j * ATTN_BLOCK, (j + 1) * ATTN_BLOCK)
            acc_n = jnp.where(low, pv_lo[r], pv_hi[r])
            den_n = pltpu.roll(jnp.where(low, pv_hi[r], pv_lo[r]), HEAD_DIM, axis=1)
            max_n = mx_ref[slot, i, j]
            if load_old is None:
                store_new(j, acc_n, max_n, den_n)
            else:
                acc_o, max_o, den_o = load_old(j)
                max_t = jnp.maximum(max_o, max_n)
                w_o = jnp.exp2(max_o - max_t)
                w_n = jnp.exp2(max_n - max_t)
                store_new(j, acc_o * w_o + acc_n * w_n, max_t, den_o * w_o + den_n * w_n)

    def windowed_bias(nb):
        if isinstance(nb, int):
            return bias_ref[min(nb, 1)]
        return bias_ref[jnp.minimum(nb, 1)]

    def prev_block(nb):
        return max(nb - 1, 0) if isinstance(nb, int) else jnp.maximum(nb - 1, 0)

    def rows1(g, i):
        nb = g * BLOCKS_PER_TRIP + i
        return nb, aligned(nb * ATTN_BLOCK), aligned(prev_block(nb) * ATTN_BLOCK)

    def score1(g, slot):
        for i in range(BLOCKS_PER_TRIP):
            nb, rows, prev_rows = rows1(g, i)
            score_stage(slot, i, q_ref, k_ref, rows, prev_rows, windowed_bias(nb))

    def value1(g, slot):
        for i in range(BLOCKS_PER_TRIP):
            _, rows, prev_rows = rows1(g, i)

            def store(j, acc, mx, den, rows=rows):
                nat_ref[ACC, j, rows, :] = acc
                nat_ref[MAX, j, rows, :] = mx
                nat_ref[DEN, j, rows, :] = den

            value_stage(slot, i, v_ref, rows, prev_rows, None, store)

    assert sorted_rows // ATTN_BLOCK == BLOCKS_PER_TRIP

    def rows2(r4, nb):
        return (aligned(r4 * sorted_rows + nb * ATTN_BLOCK),
                aligned(r4 * sorted_rows + prev_block(nb) * ATTN_BLOCK))

    def score2(r4, slot):
        for nb in range(BLOCKS_PER_TRIP):
            rows, prev_rows = rows2(r4, nb)
            score_stage(slot, nb, qs_ref, ks_ref, rows, prev_rows, windowed_bias(nb))

    def value2(r4, slot):
        for nb in range(BLOCKS_PER_TRIP):
            rows, prev_rows = rows2(r4, nb)
            nat_rows = ds(r4 + nb * (ATTN_BLOCK * SORT_STRIDE), SORT_STRIDE)

            def load(j, nat_rows=nat_rows):
                return tuple(nat_ref[t, j, nat_rows, :] for t in (ACC, MAX, DEN))

            def store(j, acc, mx, den, rows=rows):
                srt_ref[ACC, j, rows, :] = acc
                srt_ref[MAX, j, rows, :] = mx
                srt_ref[DEN, j, rows, :] = den

            value_stage(slot, nb, vs_ref, rows, prev_rows, load, store)

    def rows3(g, i):
        r16 = g * BLOCKS_PER_TRIP + i
        start = (r16 % SORT_STRIDE) * sorted_rows + r16 // SORT_STRIDE
        return r16, ds(start, SORT_STRIDE)

    def score3(g, slot):
        for i in range(BLOCKS_PER_TRIP):
            _, rows = rows3(g, i)
            score_stage(slot, i, qs_ref, ks_ref, rows, None, bias_ref[1, :, ATTN_BLOCK:])

    def value3(g, slot):
        for i in range(BLOCKS_PER_TRIP):
            r16, rows = rows3(g, i)
            out_rows = ds(r16, SORT_STRIDE * SORT_STRIDE)

            def load(j, rows=rows):
                return tuple(srt_ref[t, j, rows, :] for t in (ACC, MAX, DEN))

            def store(j, acc, mx, den, out_rows=out_rows):
                nat_ref[ACC, j, out_rows, :] = acc * (1.0 / den)

            value_stage(slot, i, vs_ref, rows, None, load, store)

    def sort_rows(i, c):
        r4 = i // (sorted_rows // ATTN_BLOCK)
        blk = i % (sorted_rows // ATTN_BLOCK)
        src = ds(r4 + blk * (ATTN_BLOCK * SORT_STRIDE), SORT_STRIDE)
        dst = aligned(i * ATTN_BLOCK)
        for j in range(SLABS_PER_KV):
            qs_ref[j, dst, :] = q_ref[j, src, :]
        ks_ref[dst, :] = k_ref[src, :]
        vs_ref[dst, :] = v_ref[src, :]
        return c

    lax.fori_loop(0, seq // ATTN_BLOCK, sort_rows, 0)

    branches = ((score1, value1, seq // ATTN_BLOCK // BLOCKS_PER_TRIP),
                (score2, value2, SORT_STRIDE),
                (score3, value3, SORT_STRIDE * SORT_STRIDE // BLOCKS_PER_TRIP))
    assert all(n % 2 == 0 for _, _, n in branches)
    branches[0][0](0, 0)
    for idx, (score, value, n_groups) in enumerate(branches):
        def body(g, c, score=score, value=value):
            value(g, g % 2)
            score(g + 1, (g + 1) % 2)
            return c

        lax.fori_loop(0, n_groups - 1, body, 0)
        value(n_groups - 1, 1)
        if idx + 1 < len(branches):
            branches[idx + 1][0](0, 0)

    out_chunk = 256

    def finish(i, c):
        r = pl.ds(pl.multiple_of(i * out_chunk, out_chunk), out_chunk)
        for j in range(SLABS_PER_KV):
            o_ref[r, j * LANES:(j + 1) * LANES] = nat_ref[ACC, j, r, :].astype(o_ref.dtype)
        return c

    lax.fori_loop(0, seq // out_chunk, finish, 0)


def _attn_bias():
    qi = np.arange(GQA_GROUP * ATTN_BLOCK)[:, None] % ATTN_BLOCK
    ki = np.arange(2 * ATTN_BLOCK)[None, :]
    own = (ki >= ATTN_BLOCK) & (ki - ATTN_BLOCK <= qi)
    valid = np.stack([own, own | ((ki < ATTN_BLOCK) & (ki >= qi))])
    return jnp.asarray(np.where(valid, 0.0, NEG_INF), F32)


def _attention(q, k, v, bsz, seq):
    n = bsz * seq
    sq = pl.Squeezed()
    return pl.pallas_call(
        functools.partial(_attn_kernel, seq=seq),
        grid=(bsz, N_KV_HEADS),
        in_specs=[
            pl.BlockSpec((SLABS_PER_KV, seq, LANES), lambda b, h: (h, b, 0)),
            pl.BlockSpec((sq, seq, LANES), lambda b, h: (h, b, 0)),
            pl.BlockSpec((sq, seq, LANES), lambda b, h: (h, b, 0)),
            pl.BlockSpec((2, GQA_GROUP * ATTN_BLOCK, 2 * ATTN_BLOCK), lambda b, h: (0, 0, 0)),
        ],
        out_specs=pl.BlockSpec((seq, SLABS_PER_KV * LANES), lambda b, h: (b, h)),
        out_shape=jax.ShapeDtypeStruct((n, ATTN_WIDTH), BF16),
        scratch_shapes=[
            pltpu.VMEM((SLABS_PER_KV, seq, LANES), F32),
            pltpu.VMEM((seq, LANES), F32),
            pltpu.VMEM((seq, LANES), F32),
            pltpu.VMEM((3, SLABS_PER_KV, seq, LANES), F32),
            pltpu.VMEM((3, SLABS_PER_KV, seq, LANES), F32),
            pltpu.VMEM((2, BLOCKS_PER_TRIP, GQA_GROUP * ATTN_BLOCK, 2 * ATTN_BLOCK), BF16),
            pltpu.VMEM((2, BLOCKS_PER_TRIP, SLABS_PER_KV, ATTN_BLOCK, LANES), F32),
        ],
        compiler_params=pltpu.CompilerParams(
            dimension_semantics=("parallel", "parallel"), vmem_limit_bytes=VMEM_LIMIT_BYTES),
        name="attn",
    )(q, k, v, _attn_bias())


def _ssd_kernel(u_ref, dt_ref, dtb_ref, aneg_ref, dskip_ref,
                tri_ref, sel_ref, expand_ref, lbias_ref,
                o_ref, state_ref, spread_ref, cst_ref, *, seq):
    j = pl.program_id(1)
    n_chunks = SSD_ROWS // CHUNK

    @pl.when(j == 0)
    def _():
        state_ref[...] = jnp.zeros(state_ref.shape, F32)

    lane = lax.broadcasted_iota(jnp.int32, (CHUNK, LANES), 1)
    low_half = lane < SSM_HEAD_DIM
    lane_group = lane // SSM_HEADS
    tri = tri_ref[...]
    lbias = lbias_ref[...]

    chunks_per_seq = seq // CHUNK

    def decay_stage(cg, slot):
        rows = pl.ds(pl.multiple_of(cg * CHUNK, CHUNK), CHUNK)
        dt = _softplus(dt_ref[rows, :] + dtb_ref[...])
        a = dt * aneg_ref[...]
        cs = sum(_dot(tri, t) for t in _split3(a))
        terms = _split3(cs) + _split3(dt)
        packed = jnp.zeros((CHUNK, LANES), BF16)
        for g, t in enumerate(terms):
            packed = jnp.where(lane_group == g, t, packed)
        spread_ref[slot] = _dot(packed, expand_ref[...])
        cs_t3 = _dot_nt(sel_ref[...], packed)
        cst_ref[slot] = (cs_t3[0:SSM_HEADS] + cs_t3[SSM_HEADS:2 * SSM_HEADS]
                         + cs_t3[2 * SSM_HEADS:3 * SSM_HEADS])

    def scan_stage(c, slot):
        rows = pl.ds(pl.multiple_of(c * CHUNK, CHUNK), CHUNK)
        x = jnp.concatenate([u_ref[s, rows, :] for s in range(X_SLABS)], axis=1)
        b_slab = X_SLABS
        c_slab = X_SLABS + SSM_GROUPS

        def spread(e):
            return spread_ref[slot, :, e * LANES:(e + 1) * LANES]

        cs_t = cst_ref[slot]
        bc_cols = SSM_HEADS * LANES
        cs_x = jnp.concatenate(
            [jnp.where(low_half, spread(2 * i), spread(2 * i + 1))
             for i in range(SSM_HEADS // 2)], axis=1)
        dt_x = spread_ref[slot, :, bc_cols:bc_cols + SSM_INNER]
        last = cs_x[CHUNK - 1:CHUNK, :]
        xdt = x * dt_x
        xdt_b = xdt.astype(BF16)
        xw_b = (xdt * jnp.exp2(last - cs_x)).astype(BF16)
        exp_cs = jnp.exp2(cs_x)
        chunk_decay = jnp.exp2(last)

        y_parts = []
        for g in range(SSM_GROUPS):
            b_g = u_ref[b_slab + g, rows, :]
            c_g = u_ref[c_slab + g, rows, :].astype(BF16)
            cb = _dot_nt(c_g, b_g.astype(BF16))
            gcols = slice(g * GROUP_WIDTH, (g + 1) * GROUP_WIDTH)
            state = state_ref[g]
            y_g = _dot(c_g, state.astype(BF16)) * exp_cs[:, gcols]
            state_ref[g] = state * chunk_decay[:, gcols] + _dot(b_g.T.astype(BF16), xw_b[:, gcols])
            diag = []
            for i in range(HEADS_PER_GROUP // 2):
                pair = g * (HEADS_PER_GROUP // 2) + i
                ms = []
                for e in (2 * pair, 2 * pair + 1):
                    diff = spread(e) - cs_t[e:e + 1, :]
                    ms.append((cb * jnp.exp2(diff + lbias)).astype(BF16))
                xs = xdt_b[:, pair * LANES:(pair + 1) * LANES]
                zero = jnp.zeros_like(xs)
                rhs = jnp.concatenate(
                    [jnp.where(low_half, xs, zero), jnp.where(low_half, zero, xs)], axis=0)
                diag.append(_dot(jnp.concatenate(ms, axis=1), rhs))
            y_parts.append(y_g + jnp.concatenate(diag, axis=1))
        o_ref[rows, :] = jnp.concatenate(y_parts, axis=1) + dskip_ref[...] * x

    @pl.when(j == 0)
    def _():
        decay_stage(0, 0)

    assert SSD_CHUNKS_PER_TRIP % 2 == 0 and n_chunks % SSD_CHUNKS_PER_TRIP == 0

    def chunk_group(g, carry):
        for i in range(SSD_CHUNKS_PER_TRIP):
            c = g * SSD_CHUNKS_PER_TRIP + i
            cg = j * n_chunks + c
            scan_stage(c, i % 2)
            decay_stage(jnp.minimum(cg + 1, chunks_per_seq - 1), (i + 1) % 2)
        return carry

    lax.fori_loop(0, n_chunks // SSD_CHUNKS_PER_TRIP, chunk_group, 0)


def _ssd_constants():
    tri = np.tril(np.ones((CHUNK, CHUNK), np.float32))
    expand = np.zeros((LANES, SSM_HEADS * LANES + SSM_INNER), np.float32)
    for lane in range(6 * SSM_HEADS):
        g, e = divmod(lane, SSM_HEADS)
        if g < 3:
            expand[lane, e * LANES:(e + 1) * LANES] = 1.0
        else:
            c0 = SSM_HEADS * LANES + e * SSM_HEAD_DIM
            expand[lane, c0:c0 + SSM_HEAD_DIM] = 1.0
    sel = np.eye(3 * SSM_HEADS, LANES, dtype=np.float32)
    lbias = np.where(np.tril(np.ones((CHUNK, CHUNK), bool)), 0.0, NEG_INF).astype(np.float32)
    return (jnp.asarray(tri, BF16), jnp.asarray(sel, BF16), jnp.asarray(expand, BF16),
            jnp.asarray(lbias, F32))


def _ssd(u, dt, dt_bias, a_neg, d_skip, bsz, seq):
    n = bsz * seq
    steps = seq // SSD_ROWS
    row = lambda b, j: (b * steps + j, 0)
    const = lambda b, j: (0, 0)
    tri, sel, expand, lbias = _ssd_constants()

    def full(a):
        return pl.BlockSpec(a.shape, const)

    consts = (dt_bias, a_neg, d_skip, tri, sel, expand, lbias)
    return pl.pallas_call(
        functools.partial(_ssd_kernel, seq=seq),
        grid=(bsz, steps),
        in_specs=[
            pl.BlockSpec((XBC_SLABS, SSD_ROWS, LANES), lambda b, j: (0, b * steps + j, 0)),
            pl.BlockSpec((seq, LANES), lambda b, j: (b, 0)),
        ] + [full(a) for a in consts],
        out_specs=pl.BlockSpec((SSD_ROWS, SSM_INNER), row),
        out_shape=jax.ShapeDtypeStruct((n, SSM_INNER), F32),
        scratch_shapes=[
            pltpu.VMEM((SSM_GROUPS, D_STATE, GROUP_WIDTH), F32),
            pltpu.VMEM((2, CHUNK, SSM_HEADS * LANES + SSM_INNER), F32),
            pltpu.VMEM((2, SSM_HEADS, LANES), F32),
        ],
        compiler_params=pltpu.CompilerParams(
            dimension_semantics=("parallel", "arbitrary"), vmem_limit_bytes=VMEM_LIMIT_BYTES),
        name="ssd",
    )(u, dt, *consts)


def _out_ffn_kernel(h_ref, attn_ref, y_ref, z_ref, sw_ref, wo_ref, nw_ref, wg_ref, wu_ref, wd_ref,
                    fw_ref, o_ref, *, final_norm):
    gated = y_ref[...] * _silu(z_ref[...])
    parts = []
    for g in range(SSM_GROUPS):
        y_g = gated[:, g * GROUP_WIDTH:(g + 1) * GROUP_WIDTH]
        ms = jnp.mean(y_g * y_g, axis=-1, keepdims=True)
        parts.append(y_g * lax.rsqrt(ms + EPS))
    y = (jnp.concatenate(parts, axis=1) * sw_ref[...]).astype(BF16)
    h1 = (h_ref[...] + _dot(attn_ref[...], wo_ref[0:ATTN_WIDTH, :])
          + _dot(y, wo_ref[ATTN_WIDTH:MIX_WIDTH, :]))
    ms = jnp.mean(h1 * h1, axis=-1, keepdims=True)
    hn = (h1 * lax.rsqrt(ms + EPS) * nw_ref[...]).astype(BF16)
    ffn = jnp.zeros_like(h1)
    for c in range(FFN_HIDDEN // FFN_CHUNK):
        cols = slice(c * FFN_CHUNK, (c + 1) * FFN_CHUNK)
        gate = _dot(hn, wg_ref[:, cols])
        up = _dot(hn, wu_ref[:, cols])
        ffn = ffn + _dot((_silu(gate) * up).astype(BF16), wd_ref[cols, :])
    acc = h1 + ffn
    if final_norm:
        ms = jnp.mean(acc * acc, axis=-1, keepdims=True)
        acc = acc * lax.rsqrt(ms + EPS) * fw_ref[...]
    o_ref[...] = acc


def _out_ffn(h, attn, y, z, ssm_w, w_out, norm_w, w_gate, w_up, w_down, final_w, final_norm):
    n = h.shape[0]
    tm = TOKEN_TILE
    row = lambda i: (i, 0)
    const = lambda i: (0, 0)

    def resident(a):
        return pl.BlockSpec(a.shape, const, pipeline_mode=pl.Buffered(1))

    return pl.pallas_call(
        functools.partial(_out_ffn_kernel, final_norm=final_norm),
        grid=(n // tm,),
        in_specs=[
            pl.BlockSpec((tm, D_MODEL), row),
            pl.BlockSpec((tm, ATTN_WIDTH), row),
            pl.BlockSpec((tm, SSM_INNER), row),
            pl.BlockSpec((tm, SSM_INNER), row),
            resident(ssm_w), resident(w_out), resident(norm_w), resident(w_gate), resident(w_up),
            resident(w_down), resident(final_w),
        ],
        out_specs=pl.BlockSpec((tm, D_MODEL), row),
        out_shape=jax.ShapeDtypeStruct((n, D_MODEL), F32),
        compiler_params=pltpu.CompilerParams(
            dimension_semantics=("parallel",), vmem_limit_bytes=VMEM_LIMIT_BYTES),
        name="out_ffn",
    )(h, attn, y, z, ssm_w, w_out, norm_w, w_gate, w_up, w_down, final_w)


def _rotary_tables(positions):
    lane = np.arange(LANES) % HEAD_DIM
    rotated = lane < ROPE_DIM
    inv_freq = ROPE_THETA ** (-jnp.arange(0, ROPE_DIM, 2, dtype=F32) / ROPE_DIM)
    freq = jnp.where(rotated, inv_freq[lane % ROPE_HALF], 0.0)
    ang = positions.astype(F32).reshape(-1, 1) * freq
    cos, sin = jnp.cos(ang), jnp.sin(ang)
    first = jnp.asarray(lane < ROPE_HALF)
    return cos, jnp.where(first, -sin, 0.0), jnp.where(first, 0.0, sin)


def _lane_slab(v):
    return jnp.tile(v.astype(F32), DT_REPLICAS).reshape(1, LANES)


def _in_proj_weight(w):
    dt0 = w.shape[1] - SSM_HEADS
    return jnp.concatenate([w[:, :dt0], jnp.tile(w[:, dt0:], (1, DT_REPLICAS))], axis=1).astype(BF16)


def kernel(x, positions, norm_mix, w_in, conv_w, conv_b, dt_bias, a_log, d_skip, ssm_norm,
           w_out, norm_ffn, w_gate, w_up, w_down, final_norm):
    bsz, seq, _ = x.shape
    depth = w_in.shape[0]
    n = bsz * seq
    assert seq % SSD_ROWS == 0 and seq % TOKEN_TILE == 0
    assert DILATED_PATTERNS == ((128, 1), (512, 4), (2048, 16)) and seq % (16 * ATTN_BLOCK) == 0
    rc, rsa, rsb = _rotary_tables(positions)
    h = x.reshape(n, D_MODEL)
    for layer in range(depth):
        q, k, v, z, u, dt = _in_proj(h, norm_mix[layer].reshape(1, D_MODEL),
                                     _in_proj_weight(w_in[layer]), rc, rsa, rsb,
                                     conv_w[layer], conv_b[layer].reshape(1, CONV_CH), seq)
        attn = _attention(q, k, v, bsz, seq)
        a_neg = -jnp.exp(a_log[layer].astype(F32)) * LOG2E
        y = _ssd(u, dt, _lane_slab(dt_bias[layer]), _lane_slab(a_neg),
                 jnp.repeat(d_skip[layer].astype(F32), SSM_HEAD_DIM).reshape(1, SSM_INNER), bsz, seq)
        h = _out_ffn(h, attn, y, z, ssm_norm[layer].reshape(1, SSM_INNER), w_out[layer].astype(BF16), norm_ffn[layer].reshape(1, D_MODEL),
                     w_gate[layer].astype(BF16), w_up[layer].astype(BF16),
                     w_down[layer].astype(BF16), final_norm.reshape(1, D_MODEL),
                     final_norm=(layer == depth - 1))
    return h.reshape(bsz, seq, D_MODEL)
```

```python
import functools
import math

import jax
import jax.numpy as jnp
import numpy as np
from jax import lax
from jax.experimental import pallas as pl
from jax.experimental.pallas import tpu as pltpu

F32 = jnp.float32
BF16 = jnp.bfloat16

D_MODEL = 1024
HEAD_DIM = 64
N_ATTN_HEADS = 8
N_KV_HEADS = 2
GQA_GROUP = N_ATTN_HEADS // N_KV_HEADS
ATTN_WIDTH = N_ATTN_HEADS * HEAD_DIM
ROPE_DIM = HEAD_DIM // 4
ROPE_HALF = ROPE_DIM // 2
ROPE_THETA = 500000.0
DILATED_PATTERNS = ((128, 1), (512, 4), (2048, 16))
ATTN_BLOCK = 128
SSM_HEAD_DIM = 64
SSM_HEADS = 16
SSM_INNER = SSM_HEADS * SSM_HEAD_DIM
SSM_GROUPS = 2
HEADS_PER_GROUP = SSM_HEADS // SSM_GROUPS
GROUP_WIDTH = SSM_INNER // SSM_GROUPS
D_STATE = 128
CONV_WIDTH = 4
CHUNK = 128
BC_WIDTH = SSM_GROUPS * D_STATE
CONV_CH = SSM_INNER + 2 * BC_WIDTH
MIX_WIDTH = ATTN_WIDTH + SSM_INNER
FFN_HIDDEN = 2816
EPS = 1e-5
LOG2E = math.log2(math.e)

LANES = 128
SUBLANES = 8
VMEM_LIMIT_BYTES = 56 * 1024 * 1024
MXU_COLS = 256

Q_SLABS = ATTN_WIDTH // LANES
HEADS_PER_SLAB = LANES // HEAD_DIM
SLABS_PER_KV = GQA_GROUP // HEADS_PER_SLAB
XBC_SLABS = CONV_CH // LANES
X_SLABS = SSM_INNER // LANES
DT_REPLICAS = LANES // SSM_HEADS
NEG_INF = float("-inf")

COL_Q = 0
COL_K = COL_Q + ATTN_WIDTH
COL_V = COL_K + N_KV_HEADS * HEAD_DIM
COL_Z = COL_V + N_KV_HEADS * HEAD_DIM
COL_XBC = COL_Z + SSM_INNER
COL_DT = COL_XBC + CONV_CH
IN_PROJ_COLS = COL_DT + LANES

TOKEN_TILE = 512
SSD_ROWS = 1024
FFN_CHUNK = 256
SORT_STRIDE = 4
BLOCKS_PER_TRIP = 4
SSD_CHUNKS_PER_TRIP = 4


def _dot(a, b):
    return jnp.dot(a, b, preferred_element_type=F32)


def _dot_nt(a, b):
    return lax.dot_general(a, b, (((1,), (1,)), ((), ())), preferred_element_type=F32)


def _split3(x):
    hi = x.astype(BF16)
    r1 = x - hi.astype(F32)
    mid = r1.astype(BF16)
    lo = (r1 - mid.astype(F32)).astype(BF16)
    return hi, mid, lo


def _silu(x):
    return x * (1.0 / (1.0 + jnp.exp(-x)))


def _softplus(x):
    return jnp.maximum(x, 0.0) + jnp.log(1.0 + jnp.exp(-jnp.abs(x)))


def _in_proj_kernel(x_ref, nw_ref, w_ref, rc_ref, rsa_ref, rsb_ref, cw_ref, cb_ref,
                    q_ref, k_ref, v_ref, z_ref, u_ref, dt_ref, pad_ref, *, tiles_per_seq):
    halo = SUBLANES

    @pl.when(pl.program_id(0) % tiles_per_seq == 0)
    def _():
        pad_ref[:, 0:halo, :] = jnp.zeros((XBC_SLABS, halo, LANES), F32)

    x = x_ref[...]
    ms = jnp.mean(x * x, axis=-1, keepdims=True)
    hn = (x * lax.rsqrt(ms + EPS) * nw_ref[...]).astype(BF16)
    rc = rc_ref[...]
    rsa = rsa_ref[...]
    rsb = rsb_ref[...]

    def rope(t):
        return (t * rc + pltpu.roll(t, LANES - ROPE_HALF, axis=1) * rsa
                + pltpu.roll(t, ROPE_HALF, axis=1) * rsb)

    def xbc_piece(p):
        c0 = COL_XBC + p * MXU_COLS
        res = _dot(hn, w_ref[:, c0:c0 + MXU_COLS])
        for i in range(MXU_COLS // LANES):
            pad_ref[p * (MXU_COLS // LANES) + i, halo:halo + TOKEN_TILE, :] = res[:, i * LANES:(i + 1) * LANES]

    def conv_piece(p):
        for s in range(p * (MXU_COLS // LANES), (p + 1) * (MXU_COLS // LANES)):
            cols = slice(s * LANES, (s + 1) * LANES)
            for c in range(TOKEN_TILE // CHUNK):
                base = c * CHUNK + halo - (CONV_WIDTH - 1)
                taps = [pad_ref[s, pl.ds(base + t, CHUNK // 2, stride=2), :]
                        for t in range(CONV_WIDTH + 1)]
                for parity in range(2):
                    u = cb_ref[:, cols] + sum(
                        taps[parity + t] * cw_ref[t:t + 1, cols] for t in range(CONV_WIDTH))
                    u_ref[s, pl.ds(c * CHUNK + parity, CHUNK // 2, stride=2), :] = _silu(u)
            pad_ref[s, 0:halo, :] = pad_ref[s, TOKEN_TILE:TOKEN_TILE + halo, :]

    def q_piece(p):
        scale = HEAD_DIM ** -0.5 * LOG2E
        res = _dot(hn, w_ref[:, COL_Q + p * MXU_COLS:COL_Q + (p + 1) * MXU_COLS])
        for i in range(MXU_COLS // LANES):
            q_ref[p * (MXU_COLS // LANES) + i] = rope(res[:, i * LANES:(i + 1) * LANES]) * scale

    def kv_piece():
        kv = _dot(hn, w_ref[:, COL_K:COL_Z])
        low = lax.broadcasted_iota(jnp.int32, (TOKEN_TILE, LANES), 1) < HEAD_DIM
        for ref, t in ((k_ref, rope(kv[:, 0:LANES])), (v_ref, kv[:, LANES:2 * LANES])):
            swapped = pltpu.roll(t, HEAD_DIM, axis=1)
            ref[0] = jnp.where(low, t, swapped)
            ref[1] = jnp.where(low, swapped, t)

    def z_piece(p):
        cols = slice(p * MXU_COLS, (p + 1) * MXU_COLS)
        z_ref[:, cols] = _dot(hn, w_ref[:, COL_Z + p * MXU_COLS:COL_Z + (p + 1) * MXU_COLS])

    def dt_piece():
        dt_ref[...] = _dot(hn, w_ref[:, COL_DT:IN_PROJ_COLS])

    assert CONV_CH // MXU_COLS == 6 and ATTN_WIDTH // MXU_COLS == 2 and SSM_INNER // MXU_COLS == 4
    xbc_piece(0)
    xbc_piece(1)
    conv_piece(0)
    xbc_piece(2)
    q_piece(0)
    conv_piece(1)
    xbc_piece(3)
    q_piece(1)
    conv_piece(2)
    xbc_piece(4)
    kv_piece()
    conv_piece(3)
    xbc_piece(5)
    z_piece(0)
    conv_piece(4)
    z_piece(1)
    z_piece(2)
    conv_piece(5)
    z_piece(3)
    dt_piece()


def _in_proj(h, norm_w, w, rc, rsa, rsb, conv_w, conv_b, seq):
    n = h.shape[0]
    tm = TOKEN_TILE
    row = lambda i: (i, 0)
    slab = lambda i: (0, i, 0)
    const = lambda i: (0, 0)
    return pl.pallas_call(
        functools.partial(_in_proj_kernel, tiles_per_seq=seq // tm),
        grid=(n // tm,),
        in_specs=[
            pl.BlockSpec((tm, D_MODEL), row),
            pl.BlockSpec((1, D_MODEL), const),
            pl.BlockSpec((D_MODEL, IN_PROJ_COLS), const),
            pl.BlockSpec((tm, LANES), row),
            pl.BlockSpec((tm, LANES), row),
            pl.BlockSpec((tm, LANES), row),
            pl.BlockSpec((CONV_WIDTH, CONV_CH), const),
            pl.BlockSpec((1, CONV_CH), const),
        ],
        out_specs=[
            pl.BlockSpec((Q_SLABS, tm, LANES), slab),
            pl.BlockSpec((N_KV_HEADS, tm, LANES), slab),
            pl.BlockSpec((N_KV_HEADS, tm, LANES), slab),
            pl.BlockSpec((tm, SSM_INNER), row),
            pl.BlockSpec((XBC_SLABS, tm, LANES), slab),
            pl.BlockSpec((tm, LANES), row),
        ],
        out_shape=[
            jax.ShapeDtypeStruct((Q_SLABS, n, LANES), F32),
            jax.ShapeDtypeStruct((N_KV_HEADS, n, LANES), F32),
            jax.ShapeDtypeStruct((N_KV_HEADS, n, LANES), F32),
            jax.ShapeDtypeStruct((n, SSM_INNER), F32),
            jax.ShapeDtypeStruct((XBC_SLABS, n, LANES), F32),
            jax.ShapeDtypeStruct((n, LANES), F32),
        ],
        scratch_shapes=[pltpu.VMEM((XBC_SLABS, tm + 2 * SUBLANES, LANES), F32)],
        compiler_params=pltpu.CompilerParams(
            dimension_semantics=("arbitrary",), vmem_limit_bytes=VMEM_LIMIT_BYTES),
        name="in_proj",
    )(h, norm_w, w, rc, rsa, rsb, conv_w, conv_b)


def _attn_kernel(q_ref, k_ref, v_ref, bias_ref, o_ref,
                 qs_ref, ks_ref, vs_ref, nat_ref, srt_ref, p_ref, mx_ref, *, seq):
    lane = lax.broadcasted_iota(jnp.int32, (ATTN_BLOCK, LANES), 1)
    low = lane < HEAD_DIM
    sorted_rows = seq // SORT_STRIDE
    half = SLABS_PER_KV * ATTN_BLOCK
    ACC, MAX, DEN = 0, 1, 2

    def ds(start, stride):
        if stride == 1:
            return pl.ds(start, ATTN_BLOCK)
        return pl.ds(start, ATTN_BLOCK, stride=stride)

    def aligned(start):
        return ds(pl.multiple_of(start, ATTN_BLOCK), 1)

    def gather(src, q_rows, prev_rows):
        cur = src[q_rows, :]
        if prev_rows is None:
            return cur
        return jnp.concatenate([src[prev_rows, :], cur], axis=0)

    def score_stage(slot, i, q_src, k_src, q_rows, prev_rows, bias):
        keys = gather(k_src, q_rows, prev_rows)
        slabs = [q_src[j, q_rows, :] for j in range(SLABS_PER_KV)]
        q4 = jnp.concatenate([jnp.where(low, t, 0.0) for t in slabs]
                             + [jnp.where(low, 0.0, t) for t in slabs], axis=0)
        s = _dot_nt(q4.astype(BF16), keys.astype(BF16)) + bias
        m_new = jnp.max(s, axis=-1, keepdims=True)
        p_ref[slot, i, :, 0:keys.shape[0]] = jnp.exp2(s - m_new).astype(BF16)
        for j in range(SLABS_PER_KV):
            lo = m_new[j * ATTN_BLOCK:(j + 1) * ATTN_BLOCK]
            hi = m_new[half + j * ATTN_BLOCK:half + (j + 1) * ATTN_BLOCK]
            mx_ref[slot, i, j] = jnp.where(low, lo, hi)

    def value_stage(slot, i, v_src, q_rows, prev_rows, load_old, store_new):
        vals = gather(v_src, q_rows, prev_rows)
        lane_k = lax.broadcasted_iota(jnp.int32, vals.shape, 1) < HEAD_DIM
        vals_lo = jnp.where(lane_k, vals, 1.0).astype(BF16)
        vals_hi = jnp.where(lane_k, 1.0, vals).astype(BF16)
        nk = vals.shape[0]
        pv_lo = _dot(p_ref[slot, i, 0:half, 0:nk], vals_lo)
        pv_hi = _dot(p_ref[slot, i, half:2 * half, 0:nk], vals_hi)
        for j in range(SLABS_PER_KV):
            r = slice(j * ATTN_BLOCK, (j + 1) * ATTN_BLOCK)
            acc_n = jnp.where(low, pv_lo[r], pv_hi[r])
            den_n = pltpu.roll(jnp.where(low, pv_hi[r], pv_lo[r]), HEAD_DIM, axis=1)
            max_n = mx_ref[slot, i, j]
            if load_old is None:
                store_new(j, acc_n, max_n, den_n)
            else:
                acc_o, max_o, den_o = load_old(j)
                max_t = jnp.maximum(max_o, max_n)
                w_o = jnp.exp2(max_o - max_t)
                w_n = jnp.exp2(max_n - max_t)
                store_new(j, acc_o * w_o + acc_n * w_n, max_t, den_o * w_o + den_n * w_n)

    def windowed_bias(nb):
        if isinstance(nb, int):
            return bias_ref[min(nb, 1)]
        return bias_ref[jnp.minimum(nb, 1)]

    def prev_block(nb):
        return max(nb - 1, 0) if isinstance(nb, int) else jnp.maximum(nb - 1, 0)

    def rows1(g, i):
        nb = g * BLOCKS_PER_TRIP + i
        return nb, aligned(nb * ATTN_BLOCK), aligned(prev_block(nb) * ATTN_BLOCK)

    def score1(g, slot):
        for i in range(BLOCKS_PER_TRIP):
            nb, rows, prev_rows = rows1(g, i)
            score_stage(slot, i, q_ref, k_ref, rows, prev_rows, windowed_bias(nb))

    def value1(g, slot):
        for i in range(BLOCKS_PER_TRIP):
            _, rows, prev_rows = rows1(g, i)

            def store(j, acc, mx, den, rows=rows):
                nat_ref[ACC, j, rows, :] = acc
                nat_ref[MAX, j, rows, :] = mx
                nat_ref[DEN, j, rows, :] = den

            value_stage(slot, i, v_ref, rows, prev_rows, None, store)

    assert sorted_rows // ATTN_BLOCK == BLOCKS_PER_TRIP

    def rows2(r4, nb):
        return (aligned(r4 * sorted_rows + nb * ATTN_BLOCK),
                aligned(r4 * sorted_rows + prev_block(nb) * ATTN_BLOCK))

    def score2(r4, slot):
        for nb in range(BLOCKS_PER_TRIP):
            rows, prev_rows = rows2(r4, nb)
            score_stage(slot, nb, qs_ref, ks_ref, rows, prev_rows, windowed_bias(nb))

    def value2(r4, slot):
        for nb in range(BLOCKS_PER_TRIP):
            rows, prev_rows = rows2(r4, nb)
            nat_rows = ds(r4 + nb * (ATTN_BLOCK * SORT_STRIDE), SORT_STRIDE)

            def load(j, nat_rows=nat_rows):
                return tuple(nat_ref[t, j, nat_rows, :] for t in (ACC, MAX, DEN))

            def store(j, acc, mx, den, rows=rows):
                srt_ref[ACC, j, rows, :] = acc
                srt_ref[MAX, j, rows, :] = mx
                srt_ref[DEN, j, rows, :] = den

            value_stage(slot, nb, vs_ref, rows, prev_rows, load, store)

    def rows3(g, i):
        r16 = g * BLOCKS_PER_TRIP + i
        start = (r16 % SORT_STRIDE) * sorted_rows + r16 // SORT_STRIDE
        return r16, ds(start, SORT_STRIDE)

    def score3(g, slot):
        for i in range(BLOCKS_PER_TRIP):
            _, rows = rows3(g, i)
            score_stage(slot, i, qs_ref, ks_ref, rows, None, bias_ref[1, :, ATTN_BLOCK:])

    def value3(g, slot):
        for i in range(BLOCKS_PER_TRIP):
            r16, rows = rows3(g, i)
            out_rows = ds(r16, SORT_STRIDE * SORT_STRIDE)

            def load(j, rows=rows):
                return tuple(srt_ref[t, j, rows, :] for t in (ACC, MAX, DEN))

            def store(j, acc, mx, den, out_rows=out_rows):
                nat_ref[ACC, j, out_rows, :] = acc * (1.0 / den)

            value_stage(slot, i, vs_ref, rows, None, load, store)

    def sort_rows(i, c):
        r4 = i // (sorted_rows // ATTN_BLOCK)
        blk = i % (sorted_rows // ATTN_BLOCK)
        src = ds(r4 + blk * (ATTN_BLOCK * SORT_STRIDE), SORT_STRIDE)
        dst = aligned(i * ATTN_BLOCK)
        for j in range(SLABS_PER_KV):
            qs_ref[j, dst, :] = q_ref[j, src, :]
        ks_ref[dst, :] = k_ref[src, :]
        vs_ref[dst, :] = v_ref[src, :]
        return c

    lax.fori_loop(0, seq // ATTN_BLOCK, sort_rows, 0)

    branches = ((score1, value1, seq // ATTN_BLOCK // BLOCKS_PER_TRIP),
                (score2, value2, SORT_STRIDE),
                (score3, value3, SORT_STRIDE * SORT_STRIDE // BLOCKS_PER_TRIP))
    assert all(n % 2 == 0 for _, _, n in branches)
    branches[0][0](0, 0)
    for idx, (score, value, n_groups) in enumerate(branches):
        def body(g, c, score=score, value=value):
            value(g, g % 2)
            score(g + 1, (g + 1) % 2)
            return c

        lax.fori_loop(0, n_groups - 1, body, 0)
        value(n_groups - 1, 1)
        if idx + 1 < len(branches):
            branches[idx + 1][0](0, 0)

    out_chunk = 256

    def finish(i, c):
        r = pl.ds(pl.multiple_of(i * out_chunk, out_chunk), out_chunk)
        for j in range(SLABS_PER_KV):
            o_ref[r, j * LANES:(j + 1) * LANES] = nat_ref[ACC, j, r, :].astype(o_ref.dtype)
        return c

    lax.fori_loop(0, seq // out_chunk, finish, 0)


def _attn_bias():
    qi = np.arange(GQA_GROUP * ATTN_BLOCK)[:, None] % ATTN_BLOCK
    ki = np.arange(2 * ATTN_BLOCK)[None, :]
    own = (ki >= ATTN_BLOCK) & (ki - ATTN_BLOCK <= qi)
    valid = np.stack([own, own | ((ki < ATTN_BLOCK) & (ki >= qi))])
    return jnp.asarray(np.where(valid, 0.0, NEG_INF), F32)


def _attention(q, k, v, bsz, seq):
    n = bsz * seq
    sq = pl.Squeezed()
    return pl.pallas_call(
        functools.partial(_attn_kernel, seq=seq),
        grid=(bsz, N_KV_HEADS),
        in_specs=[
            pl.BlockSpec((SLABS_PER_KV, seq, LANES), lambda b, h: (h, b, 0)),
            pl.BlockSpec((sq, seq, LANES), lambda b, h: (h, b, 0)),
            pl.BlockSpec((sq, seq, LANES), lambda b, h: (h, b, 0)),
            pl.BlockSpec((2, GQA_GROUP * ATTN_BLOCK, 2 * ATTN_BLOCK), lambda b, h: (0, 0, 0)),
        ],
        out_specs=pl.BlockSpec((seq, SLABS_PER_KV * LANES), lambda b, h: (b, h)),
        out_shape=jax.ShapeDtypeStruct((n, ATTN_WIDTH), BF16),
        scratch_shapes=[
            pltpu.VMEM((SLABS_PER_KV, seq, LANES), F32),
            pltpu.VMEM((seq, LANES), F32),
            pltpu.VMEM((seq, LANES), F32),
            pltpu.VMEM((3, SLABS_PER_KV, seq, LANES), F32),
            pltpu.VMEM((3, SLABS_PER_KV, seq, LANES), F32),
            pltpu.VMEM((2, BLOCKS_PER_TRIP, GQA_GROUP * ATTN_BLOCK, 2 * ATTN_BLOCK), BF16),
            pltpu.VMEM((2, BLOCKS_PER_TRIP, SLABS_PER_KV, ATTN_BLOCK, LANES), F32),
        ],
        compiler_params=pltpu.CompilerParams(
            dimension_semantics=("parallel", "parallel"), vmem_limit_bytes=VMEM_LIMIT_BYTES),
        name="attn",
    )(q, k, v, _attn_bias())


def _ssd_kernel(u_ref, dt_ref, dtb_ref, aneg_ref, dskip_ref,
                tri_ref, sel_ref, expand_ref, lbias_ref,
                o_ref, state_ref, spread_ref, cst_ref, *, seq):
    j = pl.program_id(1)
    n_chunks = SSD_ROWS // CHUNK

    @pl.when(j == 0)
    def _():
        state_ref[...] = jnp.zeros(state_ref.shape, F32)

    lane = lax.broadcasted_iota(jnp.int32, (CHUNK, LANES), 1)
    low_half = lane < SSM_HEAD_DIM
    lane_group = lane // SSM_HEADS
    tri = tri_ref[...]
    lbias = lbias_ref[...]

    chunks_per_seq = seq // CHUNK

    def decay_pieces(cg, slot):
        rows = pl.ds(pl.multiple_of(cg * CHUNK, CHUNK), CHUNK)
        dt = _softplus(dt_ref[rows, :] + dtb_ref[...])
        a = dt * aneg_ref[...]
        cs = sum(_dot(tri, t) for t in _split3(a))
        yield
        terms = _split3(cs) + _split3(dt)
        packed = jnp.zeros((CHUNK, LANES), BF16)
        for g, t in enumerate(terms):
            packed = jnp.where(lane_group == g, t, packed)
        spread_ref[slot] = _dot(packed, expand_ref[...])
        yield
        cs_t3 = _dot_nt(sel_ref[...], packed)
        cst_ref[slot] = (cs_t3[0:SSM_HEADS] + cs_t3[SSM_HEADS:2 * SSM_HEADS]
                         + cs_t3[2 * SSM_HEADS:3 * SSM_HEADS])

    def scan_pieces(c, slot):
        rows = pl.ds(pl.multiple_of(c * CHUNK, CHUNK), CHUNK)
        x = jnp.concatenate([u_ref[s, rows, :] for s in range(X_SLABS)], axis=1)
        b_slab = X_SLABS
        c_slab = X_SLABS + SSM_GROUPS

        def spread(e):
            return spread_ref[slot, :, e * LANES:(e + 1) * LANES]

        cs_t = cst_ref[slot]
        bc_cols = SSM_HEADS * LANES
        cs_x = jnp.concatenate(
            [jnp.where(low_half, spread(2 * i), spread(2 * i + 1))
             for i in range(SSM_HEADS // 2)], axis=1)
        dt_x = spread_ref[slot, :, bc_cols:bc_cols + SSM_INNER]
        last = cs_x[CHUNK - 1:CHUNK, :]
        xdt = x * dt_x
        xdt_b = xdt.astype(BF16)
        xw_b = (xdt * jnp.exp2(last - cs_x)).astype(BF16)
        exp_cs = jnp.exp2(cs_x)
        chunk_decay = jnp.exp2(last)
        yield

        y_parts = []
        for g in range(SSM_GROUPS):
            b_g = u_ref[b_slab + g, rows, :]
            c_g = u_ref[c_slab + g, rows, :].astype(BF16)
            cb = _dot_nt(c_g, b_g.astype(BF16))
            gcols = slice(g * GROUP_WIDTH, (g + 1) * GROUP_WIDTH)
            state = state_ref[g]
            y_g = _dot(c_g, state.astype(BF16)) * exp_cs[:, gcols]
            state_ref[g] = state * chunk_decay[:, gcols] + _dot(b_g.T.astype(BF16), xw_b[:, gcols])
            yield
            diag = []
            for i in range(HEADS_PER_GROUP // 2):
                pair = g * (HEADS_PER_GROUP // 2) + i
                ms = []
                for e in (2 * pair, 2 * pair + 1):
                    diff = spread(e) - cs_t[e:e + 1, :]
                    ms.append((cb * jnp.exp2(diff + lbias)).astype(BF16))
                xs = xdt_b[:, pair * LANES:(pair + 1) * LANES]
                zero = jnp.zeros_like(xs)
                rhs = jnp.concatenate(
                    [jnp.where(low_half, xs, zero), jnp.where(low_half, zero, xs)], axis=0)
                diag.append(_dot(jnp.concatenate(ms, axis=1), rhs))
                yield
            y_parts.append(y_g + jnp.concatenate(diag, axis=1))
        o_ref[rows, :] = jnp.concatenate(y_parts, axis=1) + dskip_ref[...] * x

    @pl.when(j == 0)
    def _():
        for _ in decay_pieces(0, 0):
            pass

    assert SSD_CHUNKS_PER_TRIP % 2 == 0 and n_chunks % SSD_CHUNKS_PER_TRIP == 0

    def chunk_group(g, carry):
        for i in range(SSD_CHUNKS_PER_TRIP):
            c = g * SSD_CHUNKS_PER_TRIP + i
            cg = j * n_chunks + c
            decay = decay_pieces(jnp.minimum(cg + 1, chunks_per_seq - 1), (i + 1) % 2)
            for k, _ in enumerate(scan_pieces(c, i % 2)):
                if k % 3 == 0:
                    next(decay, None)
            for _ in decay:
                pass
        return carry

    lax.fori_loop(0, n_chunks // SSD_CHUNKS_PER_TRIP, chunk_group, 0)


def _ssd_constants():
    tri = np.tril(np.ones((CHUNK, CHUNK), np.float32))
    expand = np.zeros((LANES, SSM_HEADS * LANES + SSM_INNER), np.float32)
    for lane in range(6 * SSM_HEADS):
        g, e = divmod(lane, SSM_HEADS)
        if g < 3:
            expand[lane, e * LANES:(e + 1) * LANES] = 1.0
        else:
            c0 = SSM_HEADS * LANES + e * SSM_HEAD_DIM
            expand[lane, c0:c0 + SSM_HEAD_DIM] = 1.0
    sel = np.eye(3 * SSM_HEADS, LANES, dtype=np.float32)
    lbias = np.where(np.tril(np.ones((CHUNK, CHUNK), bool)), 0.0, NEG_INF).astype(np.float32)
    return (jnp.asarray(tri, BF16), jnp.asarray(sel, BF16), jnp.asarray(expand, BF16),
            jnp.asarray(lbias, F32))


def _ssd(u, dt, dt_bias, a_neg, d_skip, bsz, seq):
    n = bsz * seq
    steps = seq // SSD_ROWS
    row = lambda b, j: (b * steps + j, 0)
    const = lambda b, j: (0, 0)
    tri, sel, expand, lbias = _ssd_constants()

    def full(a):
        return pl.BlockSpec(a.shape, const)

    consts = (dt_bias, a_neg, d_skip, tri, sel, expand, lbias)
    return pl.pallas_call(
        functools.partial(_ssd_kernel, seq=seq),
        grid=(bsz, steps),
        in_specs=[
            pl.BlockSpec((XBC_SLABS, SSD_ROWS, LANES), lambda b, j: (0, b * steps + j, 0)),
            pl.BlockSpec((seq, LANES), lambda b, j: (b, 0)),
        ] + [full(a) for a in consts],
        out_specs=pl.BlockSpec((SSD_ROWS, SSM_INNER), row),
        out_shape=jax.ShapeDtypeStruct((n, SSM_INNER), F32),
        scratch_shapes=[
            pltpu.VMEM((SSM_GROUPS, D_STATE, GROUP_WIDTH), F32),
            pltpu.VMEM((2, CHUNK, SSM_HEADS * LANES + SSM_INNER), F32),
            pltpu.VMEM((2, SSM_HEADS, LANES), F32),
        ],
        compiler_params=pltpu.CompilerParams(
            dimension_semantics=("parallel", "arbitrary"), vmem_limit_bytes=VMEM_LIMIT_BYTES),
        name="ssd",
    )(u, dt, *consts)


def _out_ffn_kernel(h_ref, attn_ref, y_ref, z_ref, sw_ref, wo_ref, nw_ref, wg_ref, wu_ref, wd_ref,
                    fw_ref, o_ref, *, final_norm):
    gated = y_ref[...] * _silu(z_ref[...])
    parts = []
    for g in range(SSM_GROUPS):
        y_g = gated[:, g * GROUP_WIDTH:(g + 1) * GROUP_WIDTH]
        ms = jnp.mean(y_g * y_g, axis=-1, keepdims=True)
        parts.append(y_g * lax.rsqrt(ms + EPS))
    y = (jnp.concatenate(parts, axis=1) * sw_ref[...]).astype(BF16)
    h1 = (h_ref[...] + _dot(attn_ref[...], wo_ref[0:ATTN_WIDTH, :])
          + _dot(y, wo_ref[ATTN_WIDTH:MIX_WIDTH, :]))
    ms = jnp.mean(h1 * h1, axis=-1, keepdims=True)
    hn = (h1 * lax.rsqrt(ms + EPS) * nw_ref[...]).astype(BF16)
    ffn = jnp.zeros_like(h1)
    for c in range(FFN_HIDDEN // FFN_CHUNK):
        cols = slice(c * FFN_CHUNK, (c + 1) * FFN_CHUNK)
        gate = _dot(hn, wg_ref[:, cols])
        up = _dot(hn, wu_ref[:, cols])
        ffn = ffn + _dot((_silu(gate) * up).astype(BF16), wd_ref[cols, :])
    acc = h1 + ffn
    if final_norm:
        ms = jnp.mean(acc * acc, axis=-1, keepdims=True)
        acc = acc * lax.rsqrt(ms + EPS) * fw_ref[...]
    o_ref[...] = acc


def _out_ffn(h, attn, y, z, ssm_w, w_out, norm_w, w_gate, w_up, w_down, final_w, final_norm):
    n = h.shape[0]
    tm = TOKEN_TILE
    row = lambda i: (i, 0)
    const = lambda i: (0, 0)

    def resident(a):
        return pl.BlockSpec(a.shape, const, pipeline_mode=pl.Buffered(1))

    return pl.pallas_call(
        functools.partial(_out_ffn_kernel, final_norm=final_norm),
        grid=(n // tm,),
        in_specs=[
            pl.BlockSpec((tm, D_MODEL), row),
            pl.BlockSpec((tm, ATTN_WIDTH), row),
            pl.BlockSpec((tm, SSM_INNER), row),
            pl.BlockSpec((tm, SSM_INNER), row),
            resident(ssm_w), resident(w_out), resident(norm_w), resident(w_gate), resident(w_up),
            resident(w_down), resident(final_w),
        ],
        out_specs=pl.BlockSpec((tm, D_MODEL), row),
        out_shape=jax.ShapeDtypeStruct((n, D_MODEL), F32),
        compiler_params=pltpu.CompilerParams(
            dimension_semantics=("parallel",), vmem_limit_bytes=VMEM_LIMIT_BYTES),
        name="out_ffn",
    )(h, attn, y, z, ssm_w, w_out, norm_w, w_gate, w_up, w_down, final_w)


def _rotary_tables(positions):
    lane = np.arange(LANES) % HEAD_DIM
    rotated = lane < ROPE_DIM
    inv_freq = ROPE_THETA ** (-jnp.arange(0, ROPE_DIM, 2, dtype=F32) / ROPE_DIM)
    freq = jnp.where(rotated, inv_freq[lane % ROPE_HALF], 0.0)
    ang = positions.astype(F32).reshape(-1, 1) * freq
    cos, sin = jnp.cos(ang), jnp.sin(ang)
    first = jnp.asarray(lane < ROPE_HALF)
    return cos, jnp.where(first, -sin, 0.0), jnp.where(first, 0.0, sin)


def _lane_slab(v):
    return jnp.tile(v.astype(F32), DT_REPLICAS).reshape(1, LANES)


def _in_proj_weight(w):
    dt0 = w.shape[1] - SSM_HEADS
    return jnp.concatenate([w[:, :dt0], jnp.tile(w[:, dt0:], (1, DT_REPLICAS))], axis=1).astype(BF16)


def kernel(x, positions, norm_mix, w_in, conv_w, conv_b, dt_bias, a_log, d_skip, ssm_norm,
           w_out, norm_ffn, w_gate, w_up, w_down, final_norm):
    bsz, seq, _ = x.shape
    depth = w_in.shape[0]
    n = bsz * seq
    assert seq % SSD_ROWS == 0 and seq % TOKEN_TILE == 0
    assert DILATED_PATTERNS == ((128, 1), (512, 4), (2048, 16)) and seq % (16 * ATTN_BLOCK) == 0
    rc, rsa, rsb = _rotary_tables(positions)
    h = x.reshape(n, D_MODEL)
    for layer in range(depth):
        q, k, v, z, u, dt = _in_proj(h, norm_mix[layer].reshape(1, D_MODEL),
                                     _in_proj_weight(w_in[layer]), rc, rsa, rsb,
                                     conv_w[layer], conv_b[layer].reshape(1, CONV_CH), seq)
        attn = _attention(q, k, v, bsz, seq)
        a_neg = -jnp.exp(a_log[layer].astype(F32)) * LOG2E
        y = _ssd(u, dt, _lane_slab(dt_bias[layer]), _lane_slab(a_neg),
                 jnp.repeat(d_skip[layer].astype(F32), SSM_HEAD_DIM).reshape(1, SSM_INNER), bsz, seq)
        h = _out_ffn(h, attn, y, z, ssm_norm[layer].reshape(1, SSM_INNER), w_out[layer].astype(BF16), norm_ffn[layer].reshape(1, D_MODEL),
                     w_gate[layer].astype(BF16), w_up[layer].astype(BF16),
                     w_down[layer].astype(BF16), final_norm.reshape(1, D_MODEL),
                     final_norm=(layer == depth - 1))
    return h.reshape(bsz, seq, D_MODEL)
```

```python
import functools
import math

import jax
import jax.numpy as jnp
import numpy as np
from jax import lax
from jax.experimental import pallas as pl
from jax.experimental.pallas import tpu as pltpu

F32 = jnp.float32
BF16 = jnp.bfloat16

D_MODEL = 1024
HEAD_DIM = 64
N_ATTN_HEADS = 8
N_KV_HEADS = 2
GQA_GROUP = N_ATTN_HEADS // N_KV_HEADS
ATTN_WIDTH = N_ATTN_HEADS * HEAD_DIM
ROPE_DIM = HEAD_DIM // 4
ROPE_HALF = ROPE_DIM // 2
ROPE_THETA = 500000.0
DILATED_PATTERNS = ((128, 1), (512, 4), (2048, 16))
ATTN_BLOCK = 128
SSM_HEAD_DIM = 64
SSM_HEADS = 16
SSM_INNER = SSM_HEADS * SSM_HEAD_DIM
SSM_GROUPS = 2
HEADS_PER_GROUP = SSM_HEADS // SSM_GROUPS
GROUP_WIDTH = SSM_INNER // SSM_GROUPS
D_STATE = 128
CONV_WIDTH = 4
CHUNK = 128
BC_WIDTH = SSM_GROUPS * D_STATE
CONV_CH = SSM_INNER + 2 * BC_WIDTH
MIX_WIDTH = ATTN_WIDTH + SSM_INNER
FFN_HIDDEN = 2816
EPS = 1e-5
LOG2E = math.log2(math.e)

LANES = 128
SUBLANES = 8
VMEM_LIMIT_BYTES = 56 * 1024 * 1024
MXU_COLS = 256
ROT_COLS = 24

Q_SLABS = ATTN_WIDTH // LANES
HEADS_PER_SLAB = LANES // HEAD_DIM
SLABS_PER_KV = GQA_GROUP // HEADS_PER_SLAB
XBC_SLABS = CONV_CH // LANES
X_SLABS = SSM_INNER // LANES
DT_REPLICAS = LANES // SSM_HEADS
NEG_INF = float("-inf")

COL_Q = 0
COL_K = COL_Q + ATTN_WIDTH
COL_V = COL_K + N_KV_HEADS * HEAD_DIM
COL_Z = COL_V + N_KV_HEADS * HEAD_DIM
COL_XBC = COL_Z + SSM_INNER
COL_DT = COL_XBC + CONV_CH
IN_PROJ_COLS = COL_DT + LANES

TOKEN_TILE = 512
SSD_ROWS = 1024
FFN_CHUNK = 256
SORT_STRIDE = 4
BLOCKS_PER_TRIP = 4
SSD_CHUNKS_PER_TRIP = 4


def _dot(a, b):
    return jnp.dot(a, b, preferred_element_type=F32)


def _dot_nt(a, b):
    return lax.dot_general(a, b, (((1,), (1,)), ((), ())), preferred_element_type=F32)


def _split3(x):
    hi = x.astype(BF16)
    r1 = x - hi.astype(F32)
    mid = r1.astype(BF16)
    lo = (r1 - mid.astype(F32)).astype(BF16)
    return hi, mid, lo


def _silu(x):
    return x * (1.0 / (1.0 + jnp.exp(-x)))


def _softplus(x):
    return jnp.maximum(x, 0.0) + jnp.log(1.0 + jnp.exp(-jnp.abs(x)))


def _in_proj_kernel(x_ref, nw_ref, w_ref, rot_ref, rot_expand_ref, cw_ref, cb_ref,
                    q_ref, k_ref, v_ref, z_ref, u_ref, dt_ref, pad_ref, *, tiles_per_seq):
    halo = SUBLANES

    @pl.when(pl.program_id(0) % tiles_per_seq == 0)
    def _():
        pad_ref[:, 0:halo, :] = jnp.zeros((XBC_SLABS, halo, LANES), F32)

    x = x_ref[...]
    ms = jnp.mean(x * x, axis=-1, keepdims=True)
    hn = (x * lax.rsqrt(ms + EPS) * nw_ref[...]).astype(BF16)
    lane = lax.broadcasted_iota(jnp.int32, (TOKEN_TILE, LANES), 1)
    hi, mid, lo = _split3(rot_ref[...])
    packed = jnp.where(lane < ROT_COLS, hi, jnp.where(lane < 2 * ROT_COLS, mid, lo))
    spread = _dot(packed, rot_expand_ref[...])
    rc = spread[:, 0:LANES]
    rsa = spread[:, LANES:2 * LANES]
    rsb = spread[:, 2 * LANES:3 * LANES]

    def rope(t):
        return (t * rc + pltpu.roll(t, LANES - ROPE_HALF, axis=1) * rsa
                + pltpu.roll(t, ROPE_HALF, axis=1) * rsb)

    def xbc_piece(p):
        c0 = COL_XBC + p * MXU_COLS
        res = _dot(hn, w_ref[:, c0:c0 + MXU_COLS])
        for i in range(MXU_COLS // LANES):
            pad_ref[p * (MXU_COLS // LANES) + i, halo:halo + TOKEN_TILE, :] = res[:, i * LANES:(i + 1) * LANES]

    def conv_piece(p):
        for s in range(p * (MXU_COLS // LANES), (p + 1) * (MXU_COLS // LANES)):
            cols = slice(s * LANES, (s + 1) * LANES)
            for c in range(TOKEN_TILE // CHUNK):
                base = c * CHUNK + halo - (CONV_WIDTH - 1)
                taps = [pad_ref[s, pl.ds(base + t, CHUNK // 2, stride=2), :]
                        for t in range(CONV_WIDTH + 1)]
                for parity in range(2):
                    u = cb_ref[:, cols] + sum(
                        taps[parity + t] * cw_ref[t:t + 1, cols] for t in range(CONV_WIDTH))
                    u_ref[s, pl.ds(c * CHUNK + parity, CHUNK // 2, stride=2), :] = _silu(u)
            pad_ref[s, 0:halo, :] = pad_ref[s, TOKEN_TILE:TOKEN_TILE + halo, :]

    def q_piece(p):
        scale = HEAD_DIM ** -0.5 * LOG2E
        res = _dot(hn, w_ref[:, COL_Q + p * MXU_COLS:COL_Q + (p + 1) * MXU_COLS])
        for i in range(MXU_COLS // LANES):
            q_ref[p * (MXU_COLS // LANES) + i] = rope(res[:, i * LANES:(i + 1) * LANES]) * scale

    def kv_piece():
        kv = _dot(hn, w_ref[:, COL_K:COL_Z])
        low = lax.broadcasted_iota(jnp.int32, (TOKEN_TILE, LANES), 1) < HEAD_DIM
        for ref, t in ((k_ref, rope(kv[:, 0:LANES])), (v_ref, kv[:, LANES:2 * LANES])):
            swapped = pltpu.roll(t, HEAD_DIM, axis=1)
            ref[0] = jnp.where(low, t, swapped)
            ref[1] = jnp.where(low, swapped, t)

    def z_piece(p):
        cols = slice(p * MXU_COLS, (p + 1) * MXU_COLS)
        z_ref[:, cols] = _dot(hn, w_ref[:, COL_Z + p * MXU_COLS:COL_Z + (p + 1) * MXU_COLS])

    def dt_piece():
        dt_ref[...] = _dot(hn, w_ref[:, COL_DT:IN_PROJ_COLS])

    assert CONV_CH // MXU_COLS == 6 and ATTN_WIDTH // MXU_COLS == 2 and SSM_INNER // MXU_COLS == 4
    xbc_piece(0)
    xbc_piece(1)
    conv_piece(0)
    xbc_piece(2)
    q_piece(0)
    conv_piece(1)
    xbc_piece(3)
    q_piece(1)
    conv_piece(2)
    xbc_piece(4)
    kv_piece()
    conv_piece(3)
    xbc_piece(5)
    z_piece(0)
    conv_piece(4)
    z_piece(1)
    z_piece(2)
    conv_piece(5)
    z_piece(3)
    dt_piece()


def _in_proj(h, norm_w, w, rot, conv_w, conv_b, seq):
    n = h.shape[0]
    tm = TOKEN_TILE
    row = lambda i: (i, 0)
    slab = lambda i: (0, i, 0)
    const = lambda i: (0, 0)
    return pl.pallas_call(
        functools.partial(_in_proj_kernel, tiles_per_seq=seq // tm),
        grid=(n // tm,),
        in_specs=[
            pl.BlockSpec((tm, D_MODEL), row),
            pl.BlockSpec((1, D_MODEL), const),
            pl.BlockSpec((D_MODEL, IN_PROJ_COLS), const),
            pl.BlockSpec((tm, LANES), row),
            pl.BlockSpec((LANES, 3 * LANES), const),
            pl.BlockSpec((CONV_WIDTH, CONV_CH), const),
            pl.BlockSpec((1, CONV_CH), const),
        ],
        out_specs=[
            pl.BlockSpec((Q_SLABS, tm, LANES), slab),
            pl.BlockSpec((N_KV_HEADS, tm, LANES), slab),
            pl.BlockSpec((N_KV_HEADS, tm, LANES), slab),
            pl.BlockSpec((tm, SSM_INNER), row),
            pl.BlockSpec((XBC_SLABS, tm, LANES), slab),
            pl.BlockSpec((tm, LANES), row),
        ],
        out_shape=[
            jax.ShapeDtypeStruct((Q_SLABS, n, LANES), F32),
            jax.ShapeDtypeStruct((N_KV_HEADS, n, LANES), F32),
            jax.ShapeDtypeStruct((N_KV_HEADS, n, LANES), F32),
            jax.ShapeDtypeStruct((n, SSM_INNER), F32),
            jax.ShapeDtypeStruct((XBC_SLABS, n, LANES), F32),
            jax.ShapeDtypeStruct((n, LANES), F32),
        ],
        scratch_shapes=[pltpu.VMEM((XBC_SLABS, tm + 2 * SUBLANES, LANES), F32)],
        compiler_params=pltpu.CompilerParams(
            dimension_semantics=("arbitrary",), vmem_limit_bytes=VMEM_LIMIT_BYTES),
        name="in_proj",
    )(h, norm_w, w, rot, _rotary_expand(), conv_w, conv_b)


def _attn_kernel(q_ref, k_ref, v_ref, bias_ref, o_ref,
                 qs_ref, ks_ref, vs_ref, nat_ref, srt_ref, p_ref, mx_ref, *, seq):
    lane = lax.broadcasted_iota(jnp.int32, (ATTN_BLOCK, LANES), 1)
    low = lane < HEAD_DIM
    sorted_rows = seq // SORT_STRIDE
    half = SLABS_PER_KV * ATTN_BLOCK
    ACC, MAX, DEN = 0, 1, 2

    def ds(start, stride):
        if stride == 1:
            return pl.ds(start, ATTN_BLOCK)
        return pl.ds(start, ATTN_BLOCK, stride=stride)

    def aligned(start):
        return ds(pl.multiple_of(start, ATTN_BLOCK), 1)

    def gather(src, q_rows, prev_rows):
        cur = src[q_rows, :]
        if prev_rows is None:
            return cur
        return jnp.concatenate([src[prev_rows, :], cur], axis=0)

    def score_stage(slot, i, q_src, k_src, q_rows, prev_rows, bias):
        keys = gather(k_src, q_rows, prev_rows)
        slabs = [q_src[j, q_rows, :] for j in range(SLABS_PER_KV)]
        q4 = jnp.concatenate([jnp.where(low, t, 0.0) for t in slabs]
                             + [jnp.where(low, 0.0, t) for t in slabs], axis=0)
        s = _dot_nt(q4.astype(BF16), keys.astype(BF16)) + bias
        m_new = jnp.max(s, axis=-1, keepdims=True)
        p_ref[slot, i, :, 0:keys.shape[0]] = jnp.exp2(s - m_new).astype(BF16)
        for j in range(SLABS_PER_KV):
            lo = m_new[j * ATTN_BLOCK:(j + 1) * ATTN_BLOCK]
            hi = m_new[half + j * ATTN_BLOCK:half + (j + 1) * ATTN_BLOCK]
            mx_ref[slot, i, j] = jnp.where(low, lo, hi)

    def value_stage(slot, i, v_src, q_rows, prev_rows, load_old, store_new):
        vals = gather(v_src, q_rows, prev_rows)
        lane_k = lax.broadcasted_iota(jnp.int32, vals.shape, 1) < HEAD_DIM
        vals_lo = jnp.where(lane_k, vals, 1.0).astype(BF16)
        vals_hi = jnp.where(lane_k, 1.0, vals).astype(BF16)
        nk = vals.shape[0]
        pv_lo = _dot(p_ref[slot, i, 0:half, 0:nk], vals_lo)
        pv_hi = _dot(p_ref[slot, i, half:2 * half, 0:nk], vals_hi)
        for j in range(SLABS_PER_KV):
            r = slice(j * ATTN_BLOCK, (j + 1) * ATTN_BLOCK)
            acc_n = jnp.where(low, pv_lo[r], pv_hi[r])
            den_n = pltpu.roll(jnp.where(low, pv_hi[r], pv_lo[r]), HEAD_DIM, axis=1)
            max_n = mx_ref[slot, i, j]
            if load_old is None:
                store_new(j, acc_n, max_n, den_n)
            else:
                acc_o, max_o, den_o = load_old(j)
                max_t = jnp.maximum(max_o, max_n)
                w_o = jnp.exp2(max_o - max_t)
                w_n = jnp.exp2(max_n - max_t)
                store_new(j, acc_o * w_o + acc_n * w_n, max_t, den_o * w_o + den_n * w_n)

    def windowed_bias(nb):
        if isinstance(nb, int):
            return bias_ref[min(nb, 1)]
        return bias_ref[jnp.minimum(nb, 1)]

    def prev_block(nb):
        return max(nb - 1, 0) if isinstance(nb, int) else jnp.maximum(nb - 1, 0)

    def rows1(g, i):
        nb = g * BLOCKS_PER_TRIP + i
        return nb, aligned(nb * ATTN_BLOCK), aligned(prev_block(nb) * ATTN_BLOCK)

    def score1(g, slot):
        for i in range(BLOCKS_PER_TRIP):
            nb, rows, prev_rows = rows1(g, i)
            score_stage(slot, i, q_ref, k_ref, rows, prev_rows, windowed_bias(nb))

    def value1(g, slot):
        for i in range(BLOCKS_PER_TRIP):
            _, rows, prev_rows = rows1(g, i)

            def store(j, acc, mx, den, rows=rows):
                nat_ref[ACC, j, rows, :] = acc
                nat_ref[MAX, j, rows, :] = mx
                nat_ref[DEN, j, rows, :] = den

            value_stage(slot, i, v_ref, rows, prev_rows, None, store)

    assert sorted_rows // ATTN_BLOCK == BLOCKS_PER_TRIP

    def rows2(r4, nb):
        return (aligned(r4 * sorted_rows + nb * ATTN_BLOCK),
                aligned(r4 * sorted_rows + prev_block(nb) * ATTN_BLOCK))

    def score2(r4, slot):
        for nb in range(BLOCKS_PER_TRIP):
            rows, prev_rows = rows2(r4, nb)
            score_stage(slot, nb, qs_ref, ks_ref, rows, prev_rows, windowed_bias(nb))

    def value2(r4, slot):
        for nb in range(BLOCKS_PER_TRIP):
            rows, prev_rows = rows2(r4, nb)
            nat_rows = ds(r4 + nb * (ATTN_BLOCK * SORT_STRIDE), SORT_STRIDE)

            def load(j, nat_rows=nat_rows):
                return tuple(nat_ref[t, j, nat_rows, :] for t in (ACC, MAX, DEN))

            def store(j, acc, mx, den, rows=rows):
                srt_ref[ACC, j, rows, :] = acc
                srt_ref[MAX, j, rows, :] = mx
                srt_ref[DEN, j, rows, :] = den

            value_stage(slot, nb, vs_ref, rows, prev_rows, load, store)

    def rows3(g, i):
        r16 = g * BLOCKS_PER_TRIP + i
        start = (r16 % SORT_STRIDE) * sorted_rows + r16 // SORT_STRIDE
        return r16, ds(start, SORT_STRIDE)

    def score3(g, slot):
        for i in range(BLOCKS_PER_TRIP):
            _, rows = rows3(g, i)
            score_stage(slot, i, qs_ref, ks_ref, rows, None, bias_ref[1, :, ATTN_BLOCK:])

    def value3(g, slot):
        for i in range(BLOCKS_PER_TRIP):
            r16, rows = rows3(g, i)
            out_rows = ds(r16, SORT_STRIDE * SORT_STRIDE)

            def load(j, rows=rows):
                return tuple(srt_ref[t, j, rows, :] for t in (ACC, MAX, DEN))

            def store(j, acc, mx, den, out_rows=out_rows):
                nat_ref[ACC, j, out_rows, :] = acc * (1.0 / den)

            value_stage(slot, i, vs_ref, rows, None, load, store)

    def sort_rows(i, c):
        r4 = i // (sorted_rows // ATTN_BLOCK)
        blk = i % (sorted_rows // ATTN_BLOCK)
        src = ds(r4 + blk * (ATTN_BLOCK * SORT_STRIDE), SORT_STRIDE)
        dst = aligned(i * ATTN_BLOCK)
        for j in range(SLABS_PER_KV):
            qs_ref[j, dst, :] = q_ref[j, src, :]
        ks_ref[dst, :] = k_ref[src, :]
        vs_ref[dst, :] = v_ref[src, :]
        return c

    lax.fori_loop(0, seq // ATTN_BLOCK, sort_rows, 0)

    branches = ((score1, value1, seq // ATTN_BLOCK // BLOCKS_PER_TRIP),
                (score2, value2, SORT_STRIDE),
                (score3, value3, SORT_STRIDE * SORT_STRIDE // BLOCKS_PER_TRIP))
    assert all(n % 2 == 0 for _, _, n in branches)
    branches[0][0](0, 0)
    for idx, (score, value, n_groups) in enumerate(branches):
        def body(g, c, score=score, value=value):
            value(g, g % 2)
            score(g + 1, (g + 1) % 2)
            return c

        lax.fori_loop(0, n_groups - 1, body, 0)
        value(n_groups - 1, 1)
        if idx + 1 < len(branches):
            branches[idx + 1][0](0, 0)

    out_chunk = 256

    def finish(i, c):
        r = pl.ds(pl.multiple_of(i * out_chunk, out_chunk), out_chunk)
        for j in range(SLABS_PER_KV):
            o_ref[r, j * LANES:(j + 1) * LANES] = nat_ref[ACC, j, r, :].astype(o_ref.dtype)
        return c

    lax.fori_loop(0, seq // out_chunk, finish, 0)


def _attn_bias():
    qi = np.arange(GQA_GROUP * ATTN_BLOCK)[:, None] % ATTN_BLOCK
    ki = np.arange(2 * ATTN_BLOCK)[None, :]
    own = (ki >= ATTN_BLOCK) & (ki - ATTN_BLOCK <= qi)
    valid = np.stack([own, own | ((ki < ATTN_BLOCK) & (ki >= qi))])
    return jnp.asarray(np.where(valid, 0.0, NEG_INF), F32)


def _attention(q, k, v, bsz, seq):
    n = bsz * seq
    sq = pl.Squeezed()
    return pl.pallas_call(
        functools.partial(_attn_kernel, seq=seq),
        grid=(bsz, N_KV_HEADS),
        in_specs=[
            pl.BlockSpec((SLABS_PER_KV, seq, LANES), lambda b, h: (h, b, 0)),
            pl.BlockSpec((sq, seq, LANES), lambda b, h: (h, b, 0)),
            pl.BlockSpec((sq, seq, LANES), lambda b, h: (h, b, 0)),
            pl.BlockSpec((2, GQA_GROUP * ATTN_BLOCK, 2 * ATTN_BLOCK), lambda b, h: (0, 0, 0)),
        ],
        out_specs=pl.BlockSpec((seq, SLABS_PER_KV * LANES), lambda b, h: (b, h)),
        out_shape=jax.ShapeDtypeStruct((n, ATTN_WIDTH), BF16),
        scratch_shapes=[
            pltpu.VMEM((SLABS_PER_KV, seq, LANES), F32),
            pltpu.VMEM((seq, LANES), F32),
            pltpu.VMEM((seq, LANES), F32),
            pltpu.VMEM((3, SLABS_PER_KV, seq, LANES), F32),
            pltpu.VMEM((3, SLABS_PER_KV, seq, LANES), F32),
            pltpu.VMEM((2, BLOCKS_PER_TRIP, GQA_GROUP * ATTN_BLOCK, 2 * ATTN_BLOCK), BF16),
            pltpu.VMEM((2, BLOCKS_PER_TRIP, SLABS_PER_KV, ATTN_BLOCK, LANES), F32),
        ],
        compiler_params=pltpu.CompilerParams(
            dimension_semantics=("parallel", "parallel"), vmem_limit_bytes=VMEM_LIMIT_BYTES),
        name="attn",
    )(q, k, v, _attn_bias())


def _ssd_kernel(u_ref, dt_ref, dtb_ref, aneg_ref, dskip_ref,
                tri_ref, sel_ref, expand_ref, lbias_ref,
                o_ref, state_ref, spread_ref, cst_ref, *, seq):
    j = pl.program_id(1)
    n_chunks = SSD_ROWS // CHUNK

    @pl.when(j == 0)
    def _():
        state_ref[...] = jnp.zeros(state_ref.shape, F32)

    lane = lax.broadcasted_iota(jnp.int32, (CHUNK, LANES), 1)
    low_half = lane < SSM_HEAD_DIM
    lane_group = lane // SSM_HEADS
    tri = tri_ref[...]
    lbias = lbias_ref[...]

    chunks_per_seq = seq // CHUNK

    def decay_pieces(cg, slot):
        rows = pl.ds(pl.multiple_of(cg * CHUNK, CHUNK), CHUNK)
        dt = _softplus(dt_ref[rows, :] + dtb_ref[...])
        a = dt * aneg_ref[...]
        cs = sum(_dot(tri, t) for t in _split3(a))
        yield
        terms = _split3(cs) + _split3(dt)
        packed = jnp.zeros((CHUNK, LANES), BF16)
        for g, t in enumerate(terms):
            packed = jnp.where(lane_group == g, t, packed)
        spread_ref[slot] = _dot(packed, expand_ref[...])
        yield
        cs_t3 = _dot_nt(sel_ref[...], packed)
        cst_ref[slot] = (cs_t3[0:SSM_HEADS] + cs_t3[SSM_HEADS:2 * SSM_HEADS]
                         + cs_t3[2 * SSM_HEADS:3 * SSM_HEADS])

    def scan_pieces(c, slot):
        rows = pl.ds(pl.multiple_of(c * CHUNK, CHUNK), CHUNK)
        x = jnp.concatenate([u_ref[s, rows, :] for s in range(X_SLABS)], axis=1)
        b_slab = X_SLABS
        c_slab = X_SLABS + SSM_GROUPS

        def spread(e):
            return spread_ref[slot, :, e * LANES:(e + 1) * LANES]

        cs_t = cst_ref[slot]
        bc_cols = SSM_HEADS * LANES
        cs_x = jnp.concatenate(
            [jnp.where(low_half, spread(2 * i), spread(2 * i + 1))
             for i in range(SSM_HEADS // 2)], axis=1)
        dt_x = spread_ref[slot, :, bc_cols:bc_cols + SSM_INNER]
        last = cs_x[CHUNK - 1:CHUNK, :]
        xdt = x * dt_x
        xdt_b = xdt.astype(BF16)
        xw_b = (xdt * jnp.exp2(last - cs_x)).astype(BF16)
        exp_cs = jnp.exp2(cs_x)
        chunk_decay = jnp.exp2(last)
        yield

        y_parts = []
        for g in range(SSM_GROUPS):
            b_g = u_ref[b_slab + g, rows, :]
            c_g = u_ref[c_slab + g, rows, :].astype(BF16)
            cb = _dot_nt(c_g, b_g.astype(BF16))
            gcols = slice(g * GROUP_WIDTH, (g + 1) * GROUP_WIDTH)
            state = state_ref[g]
            y_g = _dot(c_g, state.astype(BF16)) * exp_cs[:, gcols]
            state_ref[g] = state * chunk_decay[:, gcols] + _dot(b_g.T.astype(BF16), xw_b[:, gcols])
            yield
            diag = []
            for i in range(HEADS_PER_GROUP // 2):
                pair = g * (HEADS_PER_GROUP // 2) + i
                ms = []
                for e in (2 * pair, 2 * pair + 1):
                    diff = spread(e) - cs_t[e:e + 1, :]
                    ms.append((cb * jnp.exp2(diff + lbias)).astype(BF16))
                xs = xdt_b[:, pair * LANES:(pair + 1) * LANES]
                zero = jnp.zeros_like(xs)
                rhs = jnp.concatenate(
                    [jnp.where(low_half, xs, zero), jnp.where(low_half, zero, xs)], axis=0)
                diag.append(_dot(jnp.concatenate(ms, axis=1), rhs))
                yield
            y_parts.append(y_g + jnp.concatenate(diag, axis=1))
        o_ref[rows, :] = jnp.concatenate(y_parts, axis=1) + dskip_ref[...] * x

    @pl.when(j == 0)
    def _():
        for _ in decay_pieces(0, 0):
            pass

    assert SSD_CHUNKS_PER_TRIP % 2 == 0 and n_chunks % SSD_CHUNKS_PER_TRIP == 0

    def chunk_group(g, carry):
        for i in range(SSD_CHUNKS_PER_TRIP):
            c = g * SSD_CHUNKS_PER_TRIP + i
            cg = j * n_chunks + c
            decay = decay_pieces(jnp.minimum(cg + 1, chunks_per_seq - 1), (i + 1) % 2)
            for k, _ in enumerate(scan_pieces(c, i % 2)):
                if k % 3 == 0:
                    next(decay, None)
            for _ in decay:
                pass
        return carry

    lax.fori_loop(0, n_chunks // SSD_CHUNKS_PER_TRIP, chunk_group, 0)


def _ssd_constants():
    tri = np.tril(np.ones((CHUNK, CHUNK), np.float32))
    expand = np.zeros((LANES, SSM_HEADS * LANES + SSM_INNER), np.float32)
    for lane in range(6 * SSM_HEADS):
        g, e = divmod(lane, SSM_HEADS)
        if g < 3:
            expand[lane, e * LANES:(e + 1) * LANES] = 1.0
        else:
            c0 = SSM_HEADS * LANES + e * SSM_HEAD_DIM
            expand[lane, c0:c0 + SSM_HEAD_DIM] = 1.0
    sel = np.eye(3 * SSM_HEADS, LANES, dtype=np.float32)
    lbias = np.where(np.tril(np.ones((CHUNK, CHUNK), bool)), 0.0, NEG_INF).astype(np.float32)
    return (jnp.asarray(tri, BF16), jnp.asarray(sel, BF16), jnp.asarray(expand, BF16),
            jnp.asarray(lbias, F32))


def _ssd(u, dt, dt_bias, a_neg, d_skip, bsz, seq):
    n = bsz * seq
    steps = seq // SSD_ROWS
    row = lambda b, j: (b * steps + j, 0)
    const = lambda b, j: (0, 0)
    tri, sel, expand, lbias = _ssd_constants()

    def full(a):
        return pl.BlockSpec(a.shape, const)

    consts = (dt_bias, a_neg, d_skip, tri, sel, expand, lbias)
    return pl.pallas_call(
        functools.partial(_ssd_kernel, seq=seq),
        grid=(bsz, steps),
        in_specs=[
            pl.BlockSpec((XBC_SLABS, SSD_ROWS, LANES), lambda b, j: (0, b * steps + j, 0)),
            pl.BlockSpec((seq, LANES), lambda b, j: (b, 0)),
        ] + [full(a) for a in consts],
        out_specs=pl.BlockSpec((SSD_ROWS, SSM_INNER), row),
        out_shape=jax.ShapeDtypeStruct((n, SSM_INNER), F32),
        scratch_shapes=[
            pltpu.VMEM((SSM_GROUPS, D_STATE, GROUP_WIDTH), F32),
            pltpu.VMEM((2, CHUNK, SSM_HEADS * LANES + SSM_INNER), F32),
            pltpu.VMEM((2, SSM_HEADS, LANES), F32),
        ],
        compiler_params=pltpu.CompilerParams(
            dimension_semantics=("parallel", "arbitrary"), vmem_limit_bytes=VMEM_LIMIT_BYTES),
        name="ssd",
    )(u, dt, *consts)


def _out_ffn_kernel(h_ref, attn_ref, y_ref, z_ref, sw_ref, wo_ref, nw_ref, wg_ref, wu_ref, wd_ref,
                    fw_ref, o_ref, *, final_norm):
    gated = y_ref[...] * _silu(z_ref[...])
    parts = []
    for g in range(SSM_GROUPS):
        y_g = gated[:, g * GROUP_WIDTH:(g + 1) * GROUP_WIDTH]
        ms = jnp.mean(y_g * y_g, axis=-1, keepdims=True)
        parts.append(y_g * lax.rsqrt(ms + EPS))
    y = (jnp.concatenate(parts, axis=1) * sw_ref[...]).astype(BF16)
    h1 = (h_ref[...] + _dot(attn_ref[...], wo_ref[0:ATTN_WIDTH, :])
          + _dot(y, wo_ref[ATTN_WIDTH:MIX_WIDTH, :]))
    ms = jnp.mean(h1 * h1, axis=-1, keepdims=True)
    hn = (h1 * lax.rsqrt(ms + EPS) * nw_ref[...]).astype(BF16)
    ffn = jnp.zeros_like(h1)
    for c in range(FFN_HIDDEN // FFN_CHUNK):
        cols = slice(c * FFN_CHUNK, (c + 1) * FFN_CHUNK)
        gate = _dot(hn, wg_ref[:, cols])
        up = _dot(hn, wu_ref[:, cols])
        ffn = ffn + _dot((_silu(gate) * up).astype(BF16), wd_ref[cols, :])
    acc = h1 + ffn
    if final_norm:
        ms = jnp.mean(acc * acc, axis=-1, keepdims=True)
        acc = acc * lax.rsqrt(ms + EPS) * fw_ref[...]
    o_ref[...] = acc


def _out_ffn(h, attn, y, z, ssm_w, w_out, norm_w, w_gate, w_up, w_down, final_w, final_norm):
    n = h.shape[0]
    tm = TOKEN_TILE
    row = lambda i: (i, 0)
    const = lambda i: (0, 0)

    def resident(a):
        return pl.BlockSpec(a.shape, const, pipeline_mode=pl.Buffered(1))

    return pl.pallas_call(
        functools.partial(_out_ffn_kernel, final_norm=final_norm),
        grid=(n // tm,),
        in_specs=[
            pl.BlockSpec((tm, D_MODEL), row),
            pl.BlockSpec((tm, ATTN_WIDTH), row),
            pl.BlockSpec((tm, SSM_INNER), row),
            pl.BlockSpec((tm, SSM_INNER), row),
            resident(ssm_w), resident(w_out), resident(norm_w), resident(w_gate), resident(w_up),
            resident(w_down), resident(final_w),
        ],
        out_specs=pl.BlockSpec((tm, D_MODEL), row),
        out_shape=jax.ShapeDtypeStruct((n, D_MODEL), F32),
        compiler_params=pltpu.CompilerParams(
            dimension_semantics=("parallel",), vmem_limit_bytes=VMEM_LIMIT_BYTES),
        name="out_ffn",
    )(h, attn, y, z, ssm_w, w_out, norm_w, w_gate, w_up, w_down, final_w)


def _rotary_table(positions):
    inv_freq = ROPE_THETA ** (-jnp.arange(0, ROPE_DIM, 2, dtype=F32) / ROPE_DIM)
    ang = positions.astype(F32).reshape(-1, 1) * inv_freq
    n = ang.shape[0]
    group = jnp.concatenate([jnp.cos(ang), jnp.sin(ang), jnp.ones((n, 1), F32),
                             jnp.zeros((n, ROT_COLS - 2 * ROPE_HALF - 1), F32)], axis=1)
    return jnp.concatenate([group] * 3 + [jnp.zeros((n, LANES - 3 * ROT_COLS), F32)], axis=1)


def _rotary_expand():
    e = np.zeros((LANES, 3 * LANES), np.float32)
    for term in range(3):
        r0 = term * ROT_COLS
        for lane in range(LANES):
            c = lane % HEAD_DIM
            if c < ROPE_DIM:
                f = c % ROPE_HALF
                e[r0 + f, lane] = 1.0
                if c < ROPE_HALF:
                    e[r0 + ROPE_HALF + f, LANES + lane] = -1.0
                else:
                    e[r0 + ROPE_HALF + f, 2 * LANES + lane] = 1.0
            else:
                e[r0 + 2 * ROPE_HALF, lane] = 1.0
    return jnp.asarray(e, BF16)


def _lane_slab(v):
    return jnp.tile(v.astype(F32), DT_REPLICAS).reshape(1, LANES)


def _in_proj_weight(w):
    dt0 = w.shape[1] - SSM_HEADS
    return jnp.concatenate([w[:, :dt0], jnp.tile(w[:, dt0:], (1, DT_REPLICAS))], axis=1).astype(BF16)


def kernel(x, positions, norm_mix, w_in, conv_w, conv_b, dt_bias, a_log, d_skip, ssm_norm,
           w_out, norm_ffn, w_gate, w_up, w_down, final_norm):
    bsz, seq, _ = x.shape
    depth = w_in.shape[0]
    n = bsz * seq
    assert seq % SSD_ROWS == 0 and seq % TOKEN_TILE == 0
    assert DILATED_PATTERNS == ((128, 1), (512, 4), (2048, 16)) and seq % (16 * ATTN_BLOCK) == 0
    rot = _rotary_table(positions)
    h = x.reshape(n, D_MODEL)
    for layer in range(depth):
        q, k, v, z, u, dt = _in_proj(h, norm_mix[layer].reshape(1, D_MODEL),
                                     _in_proj_weight(w_in[layer]), rot,
                                     conv_w[layer], conv_b[layer].reshape(1, CONV_CH), seq)
        attn = _attention(q, k, v, bsz, seq)
        a_neg = -jnp.exp(a_log[layer].astype(F32)) * LOG2E
        y = _ssd(u, dt, _lane_slab(dt_bias[layer]), _lane_slab(a_neg),
                 jnp.repeat(d_skip[layer].astype(F32), SSM_HEAD_DIM).reshape(1, SSM_INNER), bsz, seq)
        h = _out_ffn(h, attn, y, z, ssm_norm[layer].reshape(1, SSM_INNER), w_out[layer].astype(BF16), norm_ffn[layer].reshape(1, D_MODEL),
                     w_gate[layer].astype(BF16), w_up[layer].astype(BF16),
                     w_down[layer].astype(BF16), final_norm.reshape(1, D_MODEL),
                     final_norm=(layer == depth - 1))
    return h.reshape(bsz, seq, D_MODEL)
```

```python
import functools
import math

import jax
import jax.numpy as jnp
import numpy as np
from jax import lax
from jax.experimental import pallas as pl
from jax.experimental.pallas import tpu as pltpu

F32 = jnp.float32
BF16 = jnp.bfloat16

D_MODEL = 1024
HEAD_DIM = 64
N_ATTN_HEADS = 8
N_KV_HEADS = 2
GQA_GROUP = N_ATTN_HEADS // N_KV_HEADS
ATTN_WIDTH = N_ATTN_HEADS * HEAD_DIM
ROPE_DIM = HEAD_DIM // 4
ROPE_HALF = ROPE_DIM // 2
ROPE_THETA = 500000.0
DILATED_PATTERNS = ((128, 1), (512, 4), (2048, 16))
ATTN_BLOCK = 128
SSM_HEAD_DIM = 64
SSM_HEADS = 16
SSM_INNER = SSM_HEADS * SSM_HEAD_DIM
SSM_GROUPS = 2
HEADS_PER_GROUP = SSM_HEADS // SSM_GROUPS
GROUP_WIDTH = SSM_INNER // SSM_GROUPS
D_STATE = 128
CONV_WIDTH = 4
CHUNK = 128
BC_WIDTH = SSM_GROUPS * D_STATE
CONV_CH = SSM_INNER + 2 * BC_WIDTH
MIX_WIDTH = ATTN_WIDTH + SSM_INNER
FFN_HIDDEN = 2816
EPS = 1e-5
LOG2E = math.log2(math.e)

LANES = 128
SUBLANES = 8
VMEM_LIMIT_BYTES = 56 * 1024 * 1024
MXU_COLS = 256
ROT_COLS = 24

Q_SLABS = ATTN_WIDTH // LANES
HEADS_PER_SLAB = LANES // HEAD_DIM
SLABS_PER_KV = GQA_GROUP // HEADS_PER_SLAB
XBC_SLABS = CONV_CH // LANES
X_SLABS = SSM_INNER // LANES
DT_REPLICAS = LANES // SSM_HEADS
NEG_INF = float("-inf")

COL_Q = 0
COL_K = COL_Q + ATTN_WIDTH
COL_V = COL_K + N_KV_HEADS * HEAD_DIM
COL_Z = COL_V + N_KV_HEADS * HEAD_DIM
COL_XBC = COL_Z + SSM_INNER
COL_DT = COL_XBC + CONV_CH
IN_PROJ_COLS = COL_DT + LANES

TOKEN_TILE = 512
SSD_ROWS = 1024
FFN_CHUNK = 256
SORT_STRIDE = 4
BLOCKS_PER_TRIP = 4
SSD_CHUNKS_PER_TRIP = 4


def _dot(a, b):
    return jnp.dot(a, b, preferred_element_type=F32)


def _dot_nt(a, b):
    return lax.dot_general(a, b, (((1,), (1,)), ((), ())), preferred_element_type=F32)


def _split3(x):
    hi = x.astype(BF16)
    r1 = x - hi.astype(F32)
    mid = r1.astype(BF16)
    lo = (r1 - mid.astype(F32)).astype(BF16)
    return hi, mid, lo


def _silu(x):
    return x * (1.0 / (1.0 + jnp.exp(-x)))


def _softplus(x):
    return jnp.maximum(x, 0.0) + jnp.log(1.0 + jnp.exp(-jnp.abs(x)))


def _in_proj_kernel(x_ref, nw_ref, w_ref, rot_ref, rot_expand_ref, cw_ref, cb_ref,
                    q_ref, k_ref, v_ref, z_ref, u_ref, dt_ref, pad_ref, *, tiles_per_seq):
    halo = SUBLANES

    @pl.when(pl.program_id(0) % tiles_per_seq == 0)
    def _():
        pad_ref[:, 0:halo, :] = jnp.zeros((XBC_SLABS, halo, LANES), F32)

    x = x_ref[...]
    ms = jnp.mean(x * x, axis=-1, keepdims=True)
    hn = (x * lax.rsqrt(ms + EPS) * nw_ref[...]).astype(BF16)
    lane = lax.broadcasted_iota(jnp.int32, (TOKEN_TILE, LANES), 1)
    hi, mid, lo = _split3(rot_ref[...])
    packed = jnp.where(lane < ROT_COLS, hi, jnp.where(lane < 2 * ROT_COLS, mid, lo))
    spread = _dot(packed, rot_expand_ref[...])
    rc = spread[:, 0:LANES]
    rsa = spread[:, LANES:2 * LANES]
    rsb = spread[:, 2 * LANES:3 * LANES]

    def rope(t):
        return (t * rc + pltpu.roll(t, LANES - ROPE_HALF, axis=1) * rsa
                + pltpu.roll(t, ROPE_HALF, axis=1) * rsb)

    def xbc_piece(p):
        c0 = COL_XBC + p * MXU_COLS
        res = _dot(hn, w_ref[:, c0:c0 + MXU_COLS])
        for i in range(MXU_COLS // LANES):
            pad_ref[p * (MXU_COLS // LANES) + i, halo:halo + TOKEN_TILE, :] = res[:, i * LANES:(i + 1) * LANES]

    def conv_piece(p):
        for s in range(p * (MXU_COLS // LANES), (p + 1) * (MXU_COLS // LANES)):
            cols = slice(s * LANES, (s + 1) * LANES)
            for c in range(TOKEN_TILE // CHUNK):
                base = c * CHUNK + halo - (CONV_WIDTH - 1)
                taps = [pad_ref[s, pl.ds(base + t, CHUNK // 2, stride=2), :]
                        for t in range(CONV_WIDTH + 1)]
                for parity in range(2):
                    u = cb_ref[:, cols] + sum(
                        taps[parity + t] * cw_ref[t:t + 1, cols] for t in range(CONV_WIDTH))
                    u_ref[s, pl.ds(c * CHUNK + parity, CHUNK // 2, stride=2), :] = _silu(u)
            pad_ref[s, 0:halo, :] = pad_ref[s, TOKEN_TILE:TOKEN_TILE + halo, :]

    def q_piece(p):
        scale = HEAD_DIM ** -0.5 * LOG2E
        res = _dot(hn, w_ref[:, COL_Q + p * MXU_COLS:COL_Q + (p + 1) * MXU_COLS])
        for i in range(MXU_COLS // LANES):
            q_ref[p * (MXU_COLS // LANES) + i] = rope(res[:, i * LANES:(i + 1) * LANES]) * scale

    def kv_piece():
        kv = _dot(hn, w_ref[:, COL_K:COL_Z])
        low = lax.broadcasted_iota(jnp.int32, (TOKEN_TILE, LANES), 1) < HEAD_DIM
        for ref, t in ((k_ref, rope(kv[:, 0:LANES])), (v_ref, kv[:, LANES:2 * LANES])):
            swapped = pltpu.roll(t, HEAD_DIM, axis=1)
            ref[0] = jnp.where(low, t, swapped)
            ref[1] = jnp.where(low, swapped, t)

    def z_piece(p):
        cols = slice(p * MXU_COLS, (p + 1) * MXU_COLS)
        z_ref[:, cols] = _dot(hn, w_ref[:, COL_Z + p * MXU_COLS:COL_Z + (p + 1) * MXU_COLS])

    def dt_piece():
        dt_ref[...] = _dot(hn, w_ref[:, COL_DT:IN_PROJ_COLS])

    assert CONV_CH // MXU_COLS == 6 and ATTN_WIDTH // MXU_COLS == 2 and SSM_INNER // MXU_COLS == 4
    xbc_piece(0)
    xbc_piece(1)
    conv_piece(0)
    xbc_piece(2)
    q_piece(0)
    conv_piece(1)
    xbc_piece(3)
    q_piece(1)
    conv_piece(2)
    xbc_piece(4)
    kv_piece()
    conv_piece(3)
    xbc_piece(5)
    z_piece(0)
    conv_piece(4)
    z_piece(1)
    z_piece(2)
    conv_piece(5)
    z_piece(3)
    dt_piece()


def _in_proj(h, norm_w, w, rot, conv_w, conv_b, seq):
    n = h.shape[0]
    tm = TOKEN_TILE
    row = lambda i: (i, 0)
    slab = lambda i: (0, i, 0)
    const = lambda i: (0, 0)
    return pl.pallas_call(
        functools.partial(_in_proj_kernel, tiles_per_seq=seq // tm),
        grid=(n // tm,),
        in_specs=[
            pl.BlockSpec((tm, D_MODEL), row),
            pl.BlockSpec((1, D_MODEL), const),
            pl.BlockSpec((D_MODEL, IN_PROJ_COLS), const),
            pl.BlockSpec((tm, LANES), row),
            pl.BlockSpec((LANES, 3 * LANES), const),
            pl.BlockSpec((CONV_WIDTH, CONV_CH), const),
            pl.BlockSpec((1, CONV_CH), const),
        ],
        out_specs=[
            pl.BlockSpec((Q_SLABS, tm, LANES), slab),
            pl.BlockSpec((N_KV_HEADS, tm, LANES), slab),
            pl.BlockSpec((N_KV_HEADS, tm, LANES), slab),
            pl.BlockSpec((tm, SSM_INNER), row),
            pl.BlockSpec((XBC_SLABS, tm, LANES), slab),
            pl.BlockSpec((tm, LANES), row),
        ],
        out_shape=[
            jax.ShapeDtypeStruct((Q_SLABS, n, LANES), F32),
            jax.ShapeDtypeStruct((N_KV_HEADS, n, LANES), F32),
            jax.ShapeDtypeStruct((N_KV_HEADS, n, LANES), F32),
            jax.ShapeDtypeStruct((n, SSM_INNER), F32),
            jax.ShapeDtypeStruct((XBC_SLABS, n, LANES), F32),
            jax.ShapeDtypeStruct((n, LANES), F32),
        ],
        scratch_shapes=[pltpu.VMEM((XBC_SLABS, tm + 2 * SUBLANES, LANES), F32)],
        compiler_params=pltpu.CompilerParams(
            dimension_semantics=("arbitrary",), vmem_limit_bytes=VMEM_LIMIT_BYTES),
        name="in_proj",
    )(h, norm_w, w, rot, _rotary_expand(), conv_w, conv_b)


def _attn_kernel(q_ref, k_ref, v_ref, bias_ref, o_ref,
                 qs_ref, ks_ref, vs_ref, nat_ref, srt_ref, p_ref, mx_ref, *, seq):
    lane = lax.broadcasted_iota(jnp.int32, (ATTN_BLOCK, LANES), 1)
    low = lane < HEAD_DIM
    sorted_rows = seq // SORT_STRIDE
    half = SLABS_PER_KV * ATTN_BLOCK
    ACC, MAX, DEN = 0, 1, 2

    def ds(start, stride):
        if stride == 1:
            return pl.ds(start, ATTN_BLOCK)
        return pl.ds(start, ATTN_BLOCK, stride=stride)

    def aligned(start):
        return ds(pl.multiple_of(start, ATTN_BLOCK), 1)

    def gather(src, q_rows, prev_rows):
        cur = src[q_rows, :]
        if prev_rows is None:
            return cur
        return jnp.concatenate([src[prev_rows, :], cur], axis=0)

    def score_stage(slot, i, q_src, k_src, q_rows, prev_rows, bias):
        keys = gather(k_src, q_rows, prev_rows)
        slabs = [q_src[j, q_rows, :] for j in range(SLABS_PER_KV)]
        q4 = jnp.concatenate([jnp.where(low, t, 0.0) for t in slabs]
                             + [jnp.where(low, 0.0, t) for t in slabs], axis=0)
        s = _dot_nt(q4.astype(BF16), keys.astype(BF16)) + bias
        m_new = jnp.max(s, axis=-1, keepdims=True)
        p_ref[slot, i, :, 0:keys.shape[0]] = jnp.exp2(s - m_new).astype(BF16)
        for j in range(SLABS_PER_KV):
            lo = m_new[j * ATTN_BLOCK:(j + 1) * ATTN_BLOCK]
            hi = m_new[half + j * ATTN_BLOCK:half + (j + 1) * ATTN_BLOCK]
            mx_ref[slot, i, j] = jnp.where(low, lo, hi)

    def value_stage(slot, i, v_src, q_rows, prev_rows, load_old, store_new):
        vals = gather(v_src, q_rows, prev_rows)
        lane_k = lax.broadcasted_iota(jnp.int32, vals.shape, 1) < HEAD_DIM
        vals_lo = jnp.where(lane_k, vals, 1.0).astype(BF16)
        vals_hi = jnp.where(lane_k, 1.0, vals).astype(BF16)
        nk = vals.shape[0]
        pv_lo = _dot(p_ref[slot, i, 0:half, 0:nk], vals_lo)
        pv_hi = _dot(p_ref[slot, i, half:2 * half, 0:nk], vals_hi)
        for j in range(SLABS_PER_KV):
            r = slice(j * ATTN_BLOCK, (j + 1) * ATTN_BLOCK)
            acc_n = jnp.where(low, pv_lo[r], pv_hi[r])
            den_n = pltpu.roll(jnp.where(low, pv_hi[r], pv_lo[r]), HEAD_DIM, axis=1)
            max_n = mx_ref[slot, i, j]
            if load_old is None:
                store_new(j, acc_n, max_n, den_n)
            else:
                acc_o, max_o, den_o = load_old(j)
                max_t = jnp.maximum(max_o, max_n)
                w_o = jnp.exp2(max_o - max_t)
                w_n = jnp.exp2(max_n - max_t)
                store_new(j, acc_o * w_o + acc_n * w_n, max_t, den_o * w_o + den_n * w_n)

    def windowed_bias(nb):
        if isinstance(nb, int):
            return bias_ref[min(nb, 1)]
        return bias_ref[jnp.minimum(nb, 1)]

    def prev_block(nb):
        return max(nb - 1, 0) if isinstance(nb, int) else jnp.maximum(nb - 1, 0)

    def rows1(g, i):
        nb = g * BLOCKS_PER_TRIP + i
        return nb, aligned(nb * ATTN_BLOCK), aligned(prev_block(nb) * ATTN_BLOCK)

    def score1(g, slot):
        for i in range(BLOCKS_PER_TRIP):
            nb, rows, prev_rows = rows1(g, i)
            score_stage(slot, i, q_ref, k_ref, rows, prev_rows, windowed_bias(nb))

    def value1(g, slot):
        for i in range(BLOCKS_PER_TRIP):
            _, rows, prev_rows = rows1(g, i)

            def store(j, acc, mx, den, rows=rows):
                nat_ref[ACC, j, rows, :] = acc
                nat_ref[MAX, j, rows, :] = mx
                nat_ref[DEN, j, rows, :] = den

            value_stage(slot, i, v_ref, rows, prev_rows, None, store)

    assert sorted_rows // ATTN_BLOCK == BLOCKS_PER_TRIP

    def rows2(r4, nb):
        return (aligned(r4 * sorted_rows + nb * ATTN_BLOCK),
                aligned(r4 * sorted_rows + prev_block(nb) * ATTN_BLOCK))

    def score2(r4, slot):
        for nb in range(BLOCKS_PER_TRIP):
            rows, prev_rows = rows2(r4, nb)
            score_stage(slot, nb, qs_ref, ks_ref, rows, prev_rows, windowed_bias(nb))

    def value2(r4, slot):
        for nb in range(BLOCKS_PER_TRIP):
            rows, prev_rows = rows2(r4, nb)
            nat_rows = ds(r4 + nb * (ATTN_BLOCK * SORT_STRIDE), SORT_STRIDE)

            def load(j, nat_rows=nat_rows):
                return tuple(nat_ref[t, j, nat_rows, :] for t in (ACC, MAX, DEN))

            def store(j, acc, mx, den, rows=rows):
                srt_ref[ACC, j, rows, :] = acc
                srt_ref[MAX, j, rows, :] = mx
                srt_ref[DEN, j, rows, :] = den

            value_stage(slot, nb, vs_ref, rows, prev_rows, load, store)

    def rows3(g, i):
        r16 = g * BLOCKS_PER_TRIP + i
        start = (r16 % SORT_STRIDE) * sorted_rows + r16 // SORT_STRIDE
        return r16, ds(start, SORT_STRIDE)

    def score3(g, slot):
        for i in range(BLOCKS_PER_TRIP):
            _, rows = rows3(g, i)
            score_stage(slot, i, qs_ref, ks_ref, rows, None, bias_ref[1, :, ATTN_BLOCK:])

    def value3(g, slot):
        for i in range(BLOCKS_PER_TRIP):
            r16, rows = rows3(g, i)
            out_rows = ds(r16, SORT_STRIDE * SORT_STRIDE)

            def load(j, rows=rows):
                return tuple(srt_ref[t, j, rows, :] for t in (ACC, MAX, DEN))

            def store(j, acc, mx, den, out_rows=out_rows):
                nat_ref[ACC, j, out_rows, :] = acc * (1.0 / den)

            value_stage(slot, i, vs_ref, rows, None, load, store)

    def sort_rows(i, c):
        r4 = i // (sorted_rows // ATTN_BLOCK)
        blk = i % (sorted_rows // ATTN_BLOCK)
        src = ds(r4 + blk * (ATTN_BLOCK * SORT_STRIDE), SORT_STRIDE)
        dst = aligned(i * ATTN_BLOCK)
        for j in range(SLABS_PER_KV):
            qs_ref[j, dst, :] = q_ref[j, src, :]
        ks_ref[dst, :] = k_ref[src, :]
        vs_ref[dst, :] = v_ref[src, :]
        return c

    lax.fori_loop(0, seq // ATTN_BLOCK, sort_rows, 0)

    branches = ((score1, value1, seq // ATTN_BLOCK // BLOCKS_PER_TRIP),
                (score2, value2, SORT_STRIDE),
                (score3, value3, SORT_STRIDE * SORT_STRIDE // BLOCKS_PER_TRIP))
    assert all(n % 2 == 0 for _, _, n in branches)
    branches[0][0](0, 0)
    for idx, (score, value, n_groups) in enumerate(branches):
        def body(g, c, score=score, value=value):
            value(g, g % 2)
            score(g + 1, (g + 1) % 2)
            return c

        lax.fori_loop(0, n_groups - 1, body, 0)
        value(n_groups - 1, 1)
        if idx + 1 < len(branches):
            branches[idx + 1][0](0, 0)

    out_chunk = 256

    def finish(i, c):
        r = pl.ds(pl.multiple_of(i * out_chunk, out_chunk), out_chunk)
        for j in range(SLABS_PER_KV):
            o_ref[r, j * LANES:(j + 1) * LANES] = nat_ref[ACC, j, r, :].astype(o_ref.dtype)
        return c

    lax.fori_loop(0, seq // out_chunk, finish, 0)


def _attn_bias():
    qi = np.arange(GQA_GROUP * ATTN_BLOCK)[:, None] % ATTN_BLOCK
    ki = np.arange(2 * ATTN_BLOCK)[None, :]
    own = (ki >= ATTN_BLOCK) & (ki - ATTN_BLOCK <= qi)
    valid = np.stack([own, own | ((ki < ATTN_BLOCK) & (ki >= qi))])
    return jnp.asarray(np.where(valid, 0.0, NEG_INF), F32)


def _attention(q, k, v, bsz, seq):
    n = bsz * seq
    sq = pl.Squeezed()
    return pl.pallas_call(
        functools.partial(_attn_kernel, seq=seq),
        grid=(bsz, N_KV_HEADS),
        in_specs=[
            pl.BlockSpec((SLABS_PER_KV, seq, LANES), lambda b, h: (h, b, 0)),
            pl.BlockSpec((sq, seq, LANES), lambda b, h: (h, b, 0)),
            pl.BlockSpec((sq, seq, LANES), lambda b, h: (h, b, 0)),
            pl.BlockSpec((2, GQA_GROUP * ATTN_BLOCK, 2 * ATTN_BLOCK), lambda b, h: (0, 0, 0)),
        ],
        out_specs=pl.BlockSpec((seq, SLABS_PER_KV * LANES), lambda b, h: (b, h)),
        out_shape=jax.ShapeDtypeStruct((n, ATTN_WIDTH), BF16),
        scratch_shapes=[
            pltpu.VMEM((SLABS_PER_KV, seq, LANES), F32),
            pltpu.VMEM((seq, LANES), F32),
            pltpu.VMEM((seq, LANES), F32),
            pltpu.VMEM((3, SLABS_PER_KV, seq, LANES), F32),
            pltpu.VMEM((3, SLABS_PER_KV, seq, LANES), F32),
            pltpu.VMEM((2, BLOCKS_PER_TRIP, GQA_GROUP * ATTN_BLOCK, 2 * ATTN_BLOCK), BF16),
            pltpu.VMEM((2, BLOCKS_PER_TRIP, SLABS_PER_KV, ATTN_BLOCK, LANES), F32),
        ],
        compiler_params=pltpu.CompilerParams(
            dimension_semantics=("parallel", "parallel"), vmem_limit_bytes=VMEM_LIMIT_BYTES),
        name="attn",
    )(q, k, v, _attn_bias())


def _ssd_kernel(u_ref, dt_ref, dtb_ref, aneg_ref, dskip_ref,
                tri_ref, sel_ref, expand_ref, lbias_ref,
                o_ref, state_ref, spread_ref, cst_ref, *, seq):
    j = pl.program_id(1)
    n_chunks = SSD_ROWS // CHUNK

    @pl.when(j == 0)
    def _():
        state_ref[...] = jnp.zeros(state_ref.shape, F32)

    lane = lax.broadcasted_iota(jnp.int32, (CHUNK, LANES), 1)
    low_half = lane < SSM_HEAD_DIM
    lane_group = lane // SSM_HEADS
    tri = tri_ref[...]
    lbias = lbias_ref[...]

    chunks_per_seq = seq // CHUNK

    def decay_pieces(cg, slot):
        rows = pl.ds(pl.multiple_of(cg * CHUNK, CHUNK), CHUNK)
        dt = _softplus(dt_ref[rows, :] + dtb_ref[...])
        a = dt * aneg_ref[...]
        cs = sum(_dot(tri, t) for t in _split3(a))
        yield
        terms = _split3(cs) + _split3(dt)
        packed = jnp.zeros((CHUNK, LANES), BF16)
        for g, t in enumerate(terms):
            packed = jnp.where(lane_group == g, t, packed)
        spread_ref[slot] = _dot(packed, expand_ref[...])
        yield
        cs_t3 = _dot_nt(sel_ref[...], packed)
        cst_ref[slot] = (cs_t3[0:SSM_HEADS] + cs_t3[SSM_HEADS:2 * SSM_HEADS]
                         + cs_t3[2 * SSM_HEADS:3 * SSM_HEADS])

    def scan_pieces(c, slot):
        rows = pl.ds(pl.multiple_of(c * CHUNK, CHUNK), CHUNK)
        x = jnp.concatenate([u_ref[s, rows, :] for s in range(X_SLABS)], axis=1)
        b_slab = X_SLABS
        c_slab = X_SLABS + SSM_GROUPS

        def spread(e):
            return spread_ref[slot, :, e * LANES:(e + 1) * LANES]

        cs_t = cst_ref[slot]
        bc_cols = SSM_HEADS * LANES
        cs_x = jnp.concatenate(
            [jnp.where(low_half, spread(2 * i), spread(2 * i + 1))
             for i in range(SSM_HEADS // 2)], axis=1)
        dt_x = spread_ref[slot, :, bc_cols:bc_cols + SSM_INNER]
        last = cs_x[CHUNK - 1:CHUNK, :]
        xdt = x * dt_x
        xdt_b = xdt.astype(BF16)
        xw_b = (xdt * jnp.exp2(last - cs_x)).astype(BF16)
        exp_cs = jnp.exp2(cs_x)
        chunk_decay = jnp.exp2(last)
        yield

        y_parts = []
        for g in range(SSM_GROUPS):
            b_g = u_ref[b_slab + g, rows, :]
            c_g = u_ref[c_slab + g, rows, :].astype(BF16)
            cb = _dot_nt(c_g, b_g.astype(BF16))
            gcols = slice(g * GROUP_WIDTH, (g + 1) * GROUP_WIDTH)
            state = state_ref[g]
            y_g = _dot(c_g, state.astype(BF16)) * exp_cs[:, gcols]
            state_ref[g] = state * chunk_decay[:, gcols] + _dot(b_g.T.astype(BF16), xw_b[:, gcols])
            yield
            diag = []
            for i in range(HEADS_PER_GROUP // 2):
                pair = g * (HEADS_PER_GROUP // 2) + i
                ms = []
                for e in (2 * pair, 2 * pair + 1):
                    diff = spread(e) - cs_t[e:e + 1, :]
                    ms.append((cb * jnp.exp2(diff + lbias)).astype(BF16))
                xs = xdt_b[:, pair * LANES:(pair + 1) * LANES]
                zero = jnp.zeros_like(xs)
                rhs = jnp.concatenate(
                    [jnp.where(low_half, xs, zero), jnp.where(low_half, zero, xs)], axis=0)
                diag.append(_dot(jnp.concatenate(ms, axis=1), rhs))
                yield
            y_parts.append(y_g + jnp.concatenate(diag, axis=1))
        o_ref[rows, :] = jnp.concatenate(y_parts, axis=1) + dskip_ref[...] * x

    @pl.when(j == 0)
    def _():
        for _ in decay_pieces(0, 0):
            pass

    assert SSD_CHUNKS_PER_TRIP % 2 == 0 and n_chunks % SSD_CHUNKS_PER_TRIP == 0

    def chunk_group(g, carry):
        for i in range(SSD_CHUNKS_PER_TRIP):
            c = g * SSD_CHUNKS_PER_TRIP + i
            cg = j * n_chunks + c
            decay = decay_pieces(jnp.minimum(cg + 1, chunks_per_seq - 1), (i + 1) % 2)
            for k, _ in enumerate(scan_pieces(c, i % 2)):
                if k % 3 == 0:
                    next(decay, None)
            for _ in decay:
                pass
        return carry

    lax.fori_loop(0, n_chunks // SSD_CHUNKS_PER_TRIP, chunk_group, 0)


def _ssd_constants():
    tri = np.tril(np.ones((CHUNK, CHUNK), np.float32))
    expand = np.zeros((LANES, SSM_HEADS * LANES + SSM_INNER), np.float32)
    for lane in range(6 * SSM_HEADS):
        g, e = divmod(lane, SSM_HEADS)
        if g < 3:
            expand[lane, e * LANES:(e + 1) * LANES] = 1.0
        else:
            c0 = SSM_HEADS * LANES + e * SSM_HEAD_DIM
            expand[lane, c0:c0 + SSM_HEAD_DIM] = 1.0
    sel = np.eye(3 * SSM_HEADS, LANES, dtype=np.float32)
    lbias = np.where(np.tril(np.ones((CHUNK, CHUNK), bool)), 0.0, NEG_INF).astype(np.float32)
    return (jnp.asarray(tri, BF16), jnp.asarray(sel, BF16), jnp.asarray(expand, BF16),
            jnp.asarray(lbias, F32))


def _ssd(u, dt, dt_bias, a_neg, d_skip, bsz, seq):
    n = bsz * seq
    steps = seq // SSD_ROWS
    row = lambda b, j: (b * steps + j, 0)
    const = lambda b, j: (0, 0)
    tri, sel, expand, lbias = _ssd_constants()

    def full(a):
        return pl.BlockSpec(a.shape, const)

    consts = (dt_bias, a_neg, d_skip, tri, sel, expand, lbias)
    return pl.pallas_call(
        functools.partial(_ssd_kernel, seq=seq),
        grid=(bsz, steps),
        in_specs=[
            pl.BlockSpec((XBC_SLABS, SSD_ROWS, LANES), lambda b, j: (0, b * steps + j, 0)),
            pl.BlockSpec((seq, LANES), lambda b, j: (b, 0)),
        ] + [full(a) for a in consts],
        out_specs=pl.BlockSpec((SSD_ROWS, SSM_INNER), row),
        out_shape=jax.ShapeDtypeStruct((n, SSM_INNER), F32),
        scratch_shapes=[
            pltpu.VMEM((SSM_GROUPS, D_STATE, GROUP_WIDTH), F32),
            pltpu.VMEM((2, CHUNK, SSM_HEADS * LANES + SSM_INNER), F32),
            pltpu.VMEM((2, SSM_HEADS, LANES), F32),
        ],
        compiler_params=pltpu.CompilerParams(
            dimension_semantics=("parallel", "arbitrary"), vmem_limit_bytes=VMEM_LIMIT_BYTES),
        name="ssd",
    )(u, dt, *consts)


def _out_ffn_kernel(h_ref, attn_ref, y_ref, z_ref, sw_ref, wo_ref, nw_ref, wg_ref, wu_ref, wd_ref,
                    fw_ref, o_ref, *, final_norm):
    gated = y_ref[...] * _silu(z_ref[...])
    parts = []
    for g in range(SSM_GROUPS):
        y_g = gated[:, g * GROUP_WIDTH:(g + 1) * GROUP_WIDTH]
        ms = jnp.mean(y_g * y_g, axis=-1, keepdims=True)
        parts.append(y_g * lax.rsqrt(ms + EPS))
    y = (jnp.concatenate(parts, axis=1) * sw_ref[...]).astype(BF16)
    h1 = (h_ref[...] + _dot(attn_ref[...], wo_ref[0:ATTN_WIDTH, :])
          + _dot(y, wo_ref[ATTN_WIDTH:MIX_WIDTH, :]))
    ms = jnp.mean(h1 * h1, axis=-1, keepdims=True)
    hn = (h1 * lax.rsqrt(ms + EPS) * nw_ref[...]).astype(BF16)
    ffn = jnp.zeros_like(h1)
    for c in range(FFN_HIDDEN // FFN_CHUNK):
        cols = slice(c * FFN_CHUNK, (c + 1) * FFN_CHUNK)
        gate = _dot(hn, wg_ref[:, cols])
        up = _dot(hn, wu_ref[:, cols])
        ffn = ffn + _dot((_silu(gate) * up).astype(BF16), wd_ref[cols, :])
    acc = h1 + ffn
    if final_norm:
        ms = jnp.mean(acc * acc, axis=-1, keepdims=True)
        acc = acc * lax.rsqrt(ms + EPS) * fw_ref[...]
    o_ref[...] = acc


def _out_ffn(h, attn, y, z, ssm_w, w_out, norm_w, w_gate, w_up, w_down, final_w, final_norm):
    n = h.shape[0]
    tm = TOKEN_TILE
    row = lambda i: (i, 0)
    const = lambda i: (0, 0)

    def resident(a):
        return pl.BlockSpec(a.shape, const, pipeline_mode=pl.Buffered(1))

    return pl.pallas_call(
        functools.partial(_out_ffn_kernel, final_norm=final_norm),
        grid=(n // tm,),
        in_specs=[
            pl.BlockSpec((tm, D_MODEL), row),
            pl.BlockSpec((tm, ATTN_WIDTH), row),
            pl.BlockSpec((tm, SSM_INNER), row),
            pl.BlockSpec((tm, SSM_INNER), row),
            resident(ssm_w), resident(w_out), resident(norm_w), resident(w_gate), resident(w_up),
            resident(w_down), resident(final_w),
        ],
        out_specs=pl.BlockSpec((tm, D_MODEL), row),
        out_shape=jax.ShapeDtypeStruct((n, D_MODEL), F32),
        compiler_params=pltpu.CompilerParams(
            dimension_semantics=("parallel",), vmem_limit_bytes=VMEM_LIMIT_BYTES),
        name="out_ffn",
    )(h, attn, y, z, ssm_w, w_out, norm_w, w_gate, w_up, w_down, final_w)


def _rotary_table(positions):
    inv_freq = ROPE_THETA ** (-jnp.arange(0, ROPE_DIM, 2, dtype=F32) / ROPE_DIM)
    lane = np.arange(LANES)
    col = np.where(lane < 3 * ROT_COLS, lane % ROT_COLS, ROT_COLS - 1)
    is_cos, is_sin, is_one = col < ROPE_HALF, (col >= ROPE_HALF) & (col < ROPE_DIM), col == ROPE_DIM
    freq = jnp.where(is_cos | is_sin, inv_freq[col % ROPE_HALF], 0.0)
    ang = positions.astype(F32).reshape(-1, 1) * freq
    return jnp.where(is_cos, jnp.cos(ang), jnp.where(is_sin, jnp.sin(ang), jnp.where(is_one, 1.0, 0.0)))


def _rotary_expand():
    e = np.zeros((LANES, 3 * LANES), np.float32)
    for term in range(3):
        r0 = term * ROT_COLS
        for lane in range(LANES):
            c = lane % HEAD_DIM
            if c < ROPE_DIM:
                f = c % ROPE_HALF
                e[r0 + f, lane] = 1.0
                if c < ROPE_HALF:
                    e[r0 + ROPE_HALF + f, LANES + lane] = -1.0
                else:
                    e[r0 + ROPE_HALF + f, 2 * LANES + lane] = 1.0
            else:
                e[r0 + 2 * ROPE_HALF, lane] = 1.0
    return jnp.asarray(e, BF16)


def _lane_slab(v):
    return jnp.tile(v.astype(F32), DT_REPLICAS).reshape(1, LANES)


def _in_proj_weight(w):
    dt0 = w.shape[1] - SSM_HEADS
    return jnp.concatenate([w[:, :dt0], jnp.tile(w[:, dt0:], (1, DT_REPLICAS))], axis=1).astype(BF16)


def kernel(x, positions, norm_mix, w_in, conv_w, conv_b, dt_bias, a_log, d_skip, ssm_norm,
           w_out, norm_ffn, w_gate, w_up, w_down, final_norm):
    bsz, seq, _ = x.shape
    depth = w_in.shape[0]
    n = bsz * seq
    assert seq % SSD_ROWS == 0 and seq % TOKEN_TILE == 0
    assert DILATED_PATTERNS == ((128, 1), (512, 4), (2048, 16)) and seq % (16 * ATTN_BLOCK) == 0
    rot = _rotary_table(positions)
    h = x.reshape(n, D_MODEL)
    for layer in range(depth):
        q, k, v, z, u, dt = _in_proj(h, norm_mix[layer].reshape(1, D_MODEL),
                                     _in_proj_weight(w_in[layer]), rot,
                                     conv_w[layer], conv_b[layer].reshape(1, CONV_CH), seq)
        attn = _attention(q, k, v, bsz, seq)
        a_neg = -jnp.exp(a_log[layer].astype(F32)) * LOG2E
        y = _ssd(u, dt, _lane_slab(dt_bias[layer]), _lane_slab(a_neg),
                 jnp.repeat(d_skip[layer].astype(F32), SSM_HEAD_DIM).reshape(1, SSM_INNER), bsz, seq)
        h = _out_ffn(h, attn, y, z, ssm_norm[layer].reshape(1, SSM_INNER), w_out[layer].astype(BF16), norm_ffn[layer].reshape(1, D_MODEL),
                     w_gate[layer].astype(BF16), w_up[layer].astype(BF16),
                     w_down[layer].astype(BF16), final_norm.reshape(1, D_MODEL),
                     final_norm=(layer == depth - 1))
    return h.reshape(bsz, seq, D_MODEL)
```
